```python
import math
import jax, jax.numpy as jnp
from jax import lax
import numpy as np

D_MODEL = 1024
BATCH = 16
SEQ = 4096
DEPTH = 4

GRID_W = 64
CTX_LEN = 256
POOL_WIDTH = 512
POOL_GROUPS = 4
POOL_WINDOWS = (2, 4, 8, 16)
GLA_HEADS = 4
GLA_DK = 128
GLA_DV = 256
QK_W = GLA_HEADS * GLA_DK
V_W = GLA_HEADS * GLA_DV
GLA_GATE_RANK = 16
GLA_GATE_TEMP = 16.0
GLA_CHUNK = 64
FFN_HIDDEN = 2816
CONV_WIDTH = 3
DEEPNORM_ALPHA = (2.0 * DEPTH) ** 0.25
DEEPNORM_BETA = (8.0 * DEPTH) ** -0.25
LN_EPS = 1e-6
IN_WIDTHS = (POOL_WIDTH, QK_W, QK_W, V_W, V_W, GLA_GATE_RANK, GLA_GATE_RANK, D_MODEL, D_MODEL)
IN_SPLITS = tuple(int(s) for s in np.cumsum(IN_WIDTHS)[:-1])
N_IN = int(sum(IN_WIDTHS))

kernel_name = "hybrid_pool_gla_convffn_dit"


def layer_norm(x, w=None, b=None):
    xf = x.astype(jnp.float32)
    mu = jnp.mean(xf, -1, keepdims=True)
    var = jnp.mean(jnp.square(xf - mu), -1, keepdims=True)
    y = (xf - mu) * lax.rsqrt(var + LN_EPS)
    if w is not None:
        y = y * w.astype(jnp.float32) + b.astype(jnp.float32)
    return y.astype(x.dtype)


def pos_embed_2d(rows, cols, dim):
    quarter = dim // 4
    omega = 1.0 / (10000.0 ** (jnp.arange(quarter, dtype=jnp.float32) / quarter))
    r = jnp.arange(rows, dtype=jnp.float32)[:, None] * omega
    cl = jnp.arange(cols, dtype=jnp.float32)[:, None] * omega
    er = jnp.concatenate([jnp.sin(r), jnp.cos(r)], -1)
    ec = jnp.concatenate([jnp.sin(cl), jnp.cos(cl)], -1)
    emb = jnp.concatenate([jnp.broadcast_to(er[:, None, :], (rows, cols, dim // 2)),
                           jnp.broadcast_to(ec[None, :, :], (rows, cols, dim // 2))], -1)
    return emb.reshape(rows * cols, dim)


def pool_minus_self(p, axis, window):
    L = p.shape[axis]
    pf = p.astype(jnp.float32)
    pad_cfg = [(0, 0)] * p.ndim
    pad_cfg[axis] = (1, 0)
    cs = jnp.pad(jnp.cumsum(pf, axis=axis), pad_cfg)
    t = jnp.arange(L)
    lo = jnp.clip(t - window // 2, 0, L)
    hi = jnp.clip(t + window // 2, 0, L)
    s = jnp.take(cs, hi, axis=axis) - jnp.take(cs, lo, axis=axis)
    shape = [1] * p.ndim
    shape[axis] = L
    cnt = (hi - lo).astype(jnp.float32).reshape(shape)
    return (s / cnt - pf).astype(p.dtype)


def pool_branch(p, grid, w_pool, pool_scale):
    B, L, Cw = p.shape
    if grid:
        view, axis = p.reshape(B, L // GRID_W, GRID_W, Cw), 2
    else:
        view, axis = p, 1
    gs = Cw // POOL_GROUPS
    ys = [pool_minus_self(view[..., g * gs:(g + 1) * gs], axis, POOL_WINDOWS[g]) for g in range(POOL_GROUPS)]
    y = jnp.stack(ys, axis=-2).reshape(B, L, POOL_GROUPS, gs)
    y = jnp.einsum('blgc,gcd->blgd', y, w_pool).reshape(B, L, Cw)
    return y * pool_scale


def dwconv(h, conv_w, conv_b, grid):
    B, L, F = h.shape
    if grid:
        view, axis = h.reshape(B, L // GRID_W, GRID_W, F), 2
    else:
        view, axis = h, 1
    n = view.shape[axis]
    half = CONV_WIDTH // 2
    pad_cfg = [(0, 0)] * view.ndim
    pad_cfg[axis] = (half, half)
    hp = jnp.pad(view, pad_cfg)
    out = sum(conv_w[k] * lax.slice_in_dim(hp, k, k + n, axis=axis) for k in range(CONV_WIDTH))
    return (out + conv_b).reshape(B, L, F)


def conv_ffn(u, grid, w_up, conv_w, conv_b, w_down):
    a, gt = jnp.split(u @ w_up, 2, axis=-1)
    a = dwconv(a, conv_w, conv_b, grid)
    return (jax.nn.gelu(a, approximate=False) * gt) @ w_down


def gla_chunk(q, k, v, log_a, s0):
    B, L, H, DK = q.shape
    DV = v.shape[-1]
    C = GLA_CHUNK
    N = L // C
    f32 = jnp.float32
    q = q.astype(f32).reshape(B, N, C, H, DK)
    k = k.astype(f32).reshape(B, N, C, H, DK)
    v = v.astype(f32).reshape(B, N, C, H, DV)
    bcum = jnp.cumsum(log_a.astype(f32).reshape(B, N, C, H, DK), axis=2)
    b_last = bcum[:, :, -1:]
    qe = q * jnp.exp(bcum)
    ke = k * jnp.exp(-bcum)
    kend = k * jnp.exp(b_last - bcum)
    mask = jnp.tril(jnp.ones((C, C), dtype=bool))
    att = jnp.where(mask, jnp.einsum('bnihd,bnjhd->bnhij', qe, ke), 0.0)
    o_intra = jnp.einsum('bnhij,bnjhv->bnihv', att, v)

    def step(S, xs):
        qe_n, kend_n, v_n, dec_n = xs
        o_n = jnp.einsum('bihd,bhdv->bihv', qe_n, S)
        S = dec_n[..., None] * S + jnp.einsum('bjhd,bjhv->bhdv', kend_n, v_n)
        return S, o_n

    xs = (jnp.moveaxis(qe, 1, 0), jnp.moveaxis(kend, 1, 0), jnp.moveaxis(v, 1, 0),
          jnp.moveaxis(jnp.exp(b_last[:, :, 0]), 1, 0))
    s_fin, o_inter = lax.scan(step, s0.astype(f32), xs)
    o = o_intra + jnp.moveaxis(o_inter, 0, 1)
    return o.reshape(B, L, H, DV), s_fin


def flip_seq(a):
    return jnp.flip(a, axis=1)


def bi_gla(q, k, v, la_f, la_b, s_f0, s_b0):
    o_f, s_f = gla_chunk(q, k, v, la_f, s_f0)
    o_b, s_b = gla_chunk(flip_seq(q), flip_seq(k), flip_seq(v), flip_seq(la_b), s_b0)
    return o_f + flip_seq(o_b), s_f, s_b


def gla_inputs(q, k, v, zf, zb, w_gate_f, b_gate_f, w_gate_b, b_gate_b):
    B, L = q.shape[:2]
    q = q.reshape(B, L, GLA_HEADS, GLA_DK) * (GLA_DK ** -0.5)
    k = k.reshape(B, L, GLA_HEADS, GLA_DK)
    v = v.reshape(B, L, GLA_HEADS, GLA_DV)
    la_f = (jax.nn.log_sigmoid((zf @ w_gate_f + b_gate_f).astype(jnp.float32)) / GLA_GATE_TEMP).reshape(B, L, GLA_HEADS, GLA_DK)
    la_b = (jax.nn.log_sigmoid((zb @ w_gate_b + b_gate_b).astype(jnp.float32)) / GLA_GATE_TEMP).reshape(B, L, GLA_HEADS, GLA_DK)
    return q, k, v, la_f, la_b


def mixer_out(pool, r, g_pool, g_gla, o, grid, w_pool, pool_scale, gla_norm_w, w_br_pool, w_br_gla, w_out):
    B, L = pool.shape[:2]
    y_pool = pool_branch(pool, grid, w_pool, pool_scale) @ w_br_pool
    of = o * lax.rsqrt(jnp.mean(jnp.square(o), -1, keepdims=True) + LN_EPS)
    of = of.reshape(B, L, V_W) * gla_norm_w.astype(jnp.float32)
    y_gla = (of.astype(r.dtype) * jax.nn.silu(r)) @ w_br_gla
    merged = jax.nn.sigmoid(g_pool) * y_pool + jax.nn.sigmoid(g_gla) * y_gla
    return merged @ w_out


def setup_inputs(seed: int = 0) -> dict:
    key = jax.random.key(seed)
    ks = jax.random.split(key, 25)
    f32 = jnp.float32

    def nrm(k, shape, scale):
        return jax.random.normal(k, shape, f32) * scale

    D, F = D_MODEL, FFN_HIDDEN
    return {
        "x": nrm(ks[0], (BATCH, SEQ, D), 1.0),
        "c": nrm(ks[1], (BATCH, D), 1.0),
        "ctx": nrm(ks[2], (BATCH, CTX_LEN, D), 1.0),
        "c_ctx": nrm(ks[3], (D,), 1.0),
        "w_mod": nrm(ks[4], (DEPTH, D, 6 * D), 0.5 * D ** -0.5),
        "b_mod": nrm(ks[5], (DEPTH, 6 * D), 0.02),
        "w_in": nrm(ks[6], (DEPTH, D, N_IN), D ** -0.5),
        "w_gate_f": nrm(ks[7], (DEPTH, GLA_GATE_RANK, QK_W), GLA_GATE_RANK ** -0.5),
        "b_gate_f": 2.0 + nrm(ks[8], (DEPTH, QK_W), 0.1),
        "w_gate_b": nrm(ks[9], (DEPTH, GLA_GATE_RANK, QK_W), GLA_GATE_RANK ** -0.5),
        "b_gate_b": 2.0 + nrm(ks[10], (DEPTH, QK_W), 0.1),
        "gla_norm_w": 1.0 + nrm(ks[11], (DEPTH, V_W), 0.02),
        "w_pool": nrm(ks[12], (DEPTH, POOL_GROUPS, POOL_WIDTH // POOL_GROUPS, POOL_WIDTH // POOL_GROUPS), (POOL_WIDTH // POOL_GROUPS) ** -0.5),
        "pool_scale": 1.0 + nrm(ks[13], (DEPTH, POOL_WIDTH), 0.02),
        "w_br_pool": nrm(ks[14], (DEPTH, POOL_WIDTH, D), POOL_WIDTH ** -0.5),
        "w_br_gla": nrm(ks[15], (DEPTH, V_W, D), V_W ** -0.5),
        "w_out": nrm(ks[16], (DEPTH, D, D), DEEPNORM_BETA * D ** -0.5),
        "ln1_w": 1.0 + nrm(ks[17], (DEPTH, D), 0.02),
        "ln1_b": nrm(ks[18], (DEPTH, D), 0.02),
        "w_up": nrm(ks[19], (DEPTH, D, 2 * F), D ** -0.5),
        "conv_w": nrm(ks[20], (DEPTH, CONV_WIDTH, F), CONV_WIDTH ** -0.5),
        "conv_b": nrm(ks[21], (DEPTH, F), 0.02),
        "w_down": nrm(ks[22], (DEPTH, F, D), DEEPNORM_BETA * F ** -0.5),
        "ln2_w": 1.0 + nrm(ks[23], (DEPTH, D), 0.02),
        "ln2_b": nrm(ks[24], (DEPTH, D), 0.02),
    }


def reference(x, c, ctx, c_ctx, w_mod, b_mod, w_in, w_gate_f, b_gate_f, w_gate_b, b_gate_b,
              gla_norm_w, w_pool, pool_scale, w_br_pool, w_br_gla, w_out, ln1_w, ln1_b,
              w_up, conv_w, conv_b, w_down, ln2_w, ln2_b):
    B, L, D = x.shape
    rows = L // GRID_W
    x = layer_norm(x + pos_embed_2d(rows, GRID_W, D).astype(x.dtype)[None])
    h = layer_norm(ctx)
    s0 = jnp.zeros((B, GLA_HEADS, GLA_DK, GLA_DV), jnp.float32)

    for l in range(DEPTH):
        last = l == DEPTH - 1
        mx = (jax.nn.silu(c) @ w_mod[l] + b_mod[l])[:, None, :]
        mc = jax.nn.silu(c_ctx) @ w_mod[l] + b_mod[l]
        sh1x, sc1x, g1x, sh2x, sc2x, g2x = jnp.split(mx, 6, axis=-1)
        sh1c, sc1c, g1c, sh2c, sc2c, g2c = jnp.split(mc, 6, axis=-1)

        ux = x * (1.0 + sc1x) + sh1x
        uc = h * (1.0 + sc1c) + sh1c
        px = jnp.split(ux @ w_in[l], IN_SPLITS, axis=-1)
        pc = jnp.split(uc @ w_in[l], IN_SPLITS, axis=-1)
        gx = gla_inputs(px[1], px[2], px[3], px[5], px[6], w_gate_f[l], b_gate_f[l], w_gate_b[l], b_gate_b[l])
        gc = gla_inputs(pc[1], pc[2], pc[3], pc[5], pc[6], w_gate_f[l], b_gate_f[l], w_gate_b[l], b_gate_b[l])
        o_c, s_f, s_b = bi_gla(*gc, s0, s0)
        o_x, _, _ = bi_gla(*gx, s_f, s_b)

        mix_x = mixer_out(px[0], px[4], px[7], px[8], o_x, True, w_pool[l], pool_scale[l],
                          gla_norm_w[l], w_br_pool[l], w_br_gla[l], w_out[l])
        x = layer_norm(DEEPNORM_ALPHA * x + g1x * mix_x, ln1_w[l], ln1_b[l])

        ux2 = x * (1.0 + sc2x) + sh2x
        x = layer_norm(DEEPNORM_ALPHA * x + g2x * conv_ffn(ux2, True, w_up[l], conv_w[l], conv_b[l], w_down[l]),
                       ln2_w[l], ln2_b[l])

        if not last:
            mix_c = mixer_out(pc[0], pc[4], pc[7], pc[8], o_c, False, w_pool[l], pool_scale[l],
                              gla_norm_w[l], w_br_pool[l], w_br_gla[l], w_out[l])
            h = layer_norm(DEEPNORM_ALPHA * h + g1c * mix_c, ln1_w[l], ln1_b[l])
            uc2 = h * (1.0 + sc2c) + sh2c
            h = layer_norm(DEEPNORM_ALPHA * h + g2c * conv_ffn(uc2, False, w_up[l], conv_w[l], conv_b[l], w_down[l]),
                           ln2_w[l], ln2_b[l])
    return x
```

```python
import functools

import numpy as np
import jax
import jax.numpy as jnp
from jax import lax
from jax.experimental import pallas as pl
from jax.experimental.pallas import tpu as pltpu

f32 = jnp.float32
bf16 = jnp.bfloat16

D_MODEL = 1024
GRID_W = 64
POOL_WIDTH = 512
POOL_GROUPS = 4
POOL_GS = POOL_WIDTH // POOL_GROUPS
POOL_WINDOWS = (2, 4, 8, 16)
GLA_HEADS = 4
GLA_DK = 128
GLA_DV = 256
QK_W = GLA_HEADS * GLA_DK
V_W = GLA_HEADS * GLA_DV
GLA_GATE_RANK = 16
GLA_GATE_TEMP = 16.0
FFN_HIDDEN = 2816
LN_EPS = 1e-6

LANES = 128
POOL_BLOCK = 256
GLA_CHUNK = 128
GLA_BLOCK = 256
FFN_CHUNK = 256
TOKEN_TILE = 512
VMEM_LIMIT = 56 * 1024 * 1024

C_POOL, C_Q, C_K, C_V, C_R, C_GP, C_GG, C_Z = 0, 512, 1024, 1536, 2560, 3584, 4608, 5632
N_PACKED = C_Z + LANES


def _sigmoid(t):
    return 1.0 / (1.0 + jnp.exp(-t))


def _layer_norm(t):
    mu = jnp.mean(t, axis=-1, keepdims=True)
    d = t - mu
    var = jnp.mean(d * d, axis=-1, keepdims=True)
    return d * lax.rsqrt(var + LN_EPS)


def _split_bf16(t):
    hi = t.astype(bf16)
    lo = (t - hi.astype(f32)).astype(bf16)
    return hi, lo


def _dot(a, b):
    return jnp.dot(a, b, preferred_element_type=f32)


def _mod_kernel(c_ref, w_ref, b_ref, o_ref):
    cs = c_ref[...]
    s = cs * _sigmoid(cs)
    o_ref[0, 0] = jnp.dot(s, w_ref[0], preferred_element_type=f32,
                          precision=lax.Precision.HIGHEST) + b_ref[0]


def _modulation(cm, w_mod, b_mod):
    depth, d, _ = w_mod.shape
    rows = cm.shape[0]
    return pl.pallas_call(
        _mod_kernel,
        grid=(depth, 6),
        in_specs=[
            pl.BlockSpec((rows, d), lambda l, s: (0, 0)),
            pl.BlockSpec((1, d, d), lambda l, s: (l, 0, s)),
            pl.BlockSpec((1, 1, d), lambda l, s: (l, 0, s)),
        ],
        out_specs=pl.BlockSpec((1, 1, rows, d), lambda l, s: (l, s, 0, 0)),
        out_shape=jax.ShapeDtypeStruct((depth, 6, rows, d), f32),
        compiler_params=pltpu.CompilerParams(
            dimension_semantics=("arbitrary", "arbitrary"), vmem_limit_bytes=VMEM_LIMIT),
        name="modulation",
    )(cm, w_mod, b_mod.reshape(depth, 1, 6 * d))


def _embed_kernel(ctx_ref, x_ref, pos_ref, o_ref, *, nct):
    i = pl.program_id(0)

    @pl.when(i < nct)
    def _():
        o_ref[...] = _layer_norm(ctx_ref[...])

    @pl.when(i >= nct)
    def _():
        o_ref[...] = _layer_norm(x_ref[...] + pos_ref[...])


def _embed(ctx_flat, x_flat, pos, tm):
    d = x_flat.shape[1]
    nct = ctx_flat.shape[0] // tm
    nlt = x_flat.shape[0] // tm
    npos = pos.shape[0] // tm
    return pl.pallas_call(
        functools.partial(_embed_kernel, nct=nct),
        grid=(nct + nlt,),
        in_specs=[
            pl.BlockSpec((tm, d), lambda i: (jnp.minimum(i, nct - 1), 0)),
            pl.BlockSpec((tm, d), lambda i: (jnp.maximum(i - nct, 0), 0)),
            pl.BlockSpec((tm, d), lambda i: (jnp.maximum(i - nct, 0) % npos, 0)),
        ],
        out_specs=pl.BlockSpec((tm, d), lambda i: (i, 0)),
        out_shape=jax.ShapeDtypeStruct(((nct + nlt) * tm, d), f32),
        compiler_params=pltpu.CompilerParams(
            dimension_semantics=("arbitrary",), vmem_limit_bytes=VMEM_LIMIT),
        name="embed",
    )(ctx_flat, x_flat, pos)


def _in_proj_kernel(x_ref, sh_ref, sc_ref, w_ref, pm_ref, cnt_ref, wpool_ref, pscale_ref,
                    pb_ref, q_ref, k_ref, v_ref, rs_ref, sgp_ref, sgg_ref, z_ref, *, tm):
    u = (x_ref[...] * (1.0 + sc_ref[0]) + sh_ref[0]).astype(bf16)

    def proj(a, n):
        return _dot(u, w_ref[:, a:a + n])

    p = proj(C_POOL, POOL_WIDTH)
    for sb in range(tm // POOL_BLOCK):
        rows = slice(sb * POOL_BLOCK, (sb + 1) * POOL_BLOCK)
        for g in range(POOL_GROUPS):
            cols = slice(g * POOL_GS, (g + 1) * POOL_GS)
            pg = p[rows, cols]
            hi, lo = _split_bf16(pg)
            band = pm_ref[0, g]
            s = _dot(band, hi) + _dot(band, lo)
            y = s / cnt_ref[0, :, cols] - pg
            yw = _dot(y.astype(bf16), wpool_ref[g]) * pscale_ref[:, cols]
            pb_ref[rows, cols] = yw.astype(bf16)

    half = 512
    q_ref[...] = (proj(C_Q, QK_W) * (GLA_DK ** -0.5)).astype(bf16)
    k_ref[...] = proj(C_K, QK_W).astype(bf16)
    for j in range(V_W // half):
        cs = slice(j * half, (j + 1) * half)
        v_ref[:, cs] = proj(C_V + j * half, half).astype(bf16)
        r = proj(C_R + j * half, half)
        rs_ref[:, cs] = (r * _sigmoid(r)).astype(bf16)
        sgp_ref[:, cs] = _sigmoid(proj(C_GP + j * half, half)).astype(bf16)
        sgg_ref[:, cs] = _sigmoid(proj(C_GG + j * half, half)).astype(bf16)
    z_ref[...] = proj(C_Z, LANES)


def _in_proj(xs, mod, w_in_p, pmats, cnts, w_pool_b, pool_scale, *, layer, tm, nct, tpb, nb, mod_rows):
    n, d = xs.shape
    nt = n // tm

    def mod_idx(slot):
        def idx(i):
            row = jnp.where(i < nct, nb, jnp.maximum(i - nct, 0) // tpb)
            return ((layer * 6 + slot) * mod_rows + row, 0, 0)
        return idx

    def kind(i):
        return jnp.where(i < nct, 1, 0)

    tok = lambda w: pl.BlockSpec((tm, w), lambda i: (i, 0))
    const1 = pl.Buffered(1)
    out_shapes = (
        jax.ShapeDtypeStruct((n, POOL_WIDTH), bf16),
        jax.ShapeDtypeStruct((n, QK_W), bf16),
        jax.ShapeDtypeStruct((n, QK_W), bf16),
        jax.ShapeDtypeStruct((n, V_W), bf16),
        jax.ShapeDtypeStruct((n, V_W), bf16),
        jax.ShapeDtypeStruct((n, D_MODEL), bf16),
        jax.ShapeDtypeStruct((n, D_MODEL), bf16),
        jax.ShapeDtypeStruct((n, LANES), f32),
    )
    return pl.pallas_call(
        functools.partial(_in_proj_kernel, tm=tm),
        grid=(nt,),
        in_specs=[
            tok(d),
            pl.BlockSpec((1, 1, d), mod_idx(0)),
            pl.BlockSpec((1, 1, d), mod_idx(1)),
            pl.BlockSpec((None, d, N_PACKED), lambda i: (layer, 0, 0), pipeline_mode=const1),
            pl.BlockSpec((1, POOL_GROUPS, POOL_BLOCK, POOL_BLOCK), lambda i: (kind(i), 0, 0, 0)),
            pl.BlockSpec((1, POOL_BLOCK, POOL_WIDTH), lambda i: (kind(i), 0, 0)),
            pl.BlockSpec((None, POOL_GROUPS, POOL_GS, POOL_GS), lambda i: (layer, 0, 0, 0)),
            pl.BlockSpec((None, 1, POOL_WIDTH), lambda i: (layer, 0, 0)),
        ],
        out_specs=[tok(POOL_WIDTH), tok(QK_W), tok(QK_W), tok(V_W), tok(V_W),
                   tok(D_MODEL), tok(D_MODEL), tok(LANES)],
        out_shape=out_shapes,
        compiler_params=pltpu.CompilerParams(
            dimension_semantics=("arbitrary",), vmem_limit_bytes=VMEM_LIMIT),
        name="in_proj",
    )(xs, mod, mod, w_in_p, pmats, cnts, w_pool_b, pool_scale)


def _gla_direction(q_ref, k_ref, v_ref, z_ref, wg, bg, o_ref, s_ref, *, reverse, nchunks):
    c = GLA_CHUNK
    row = lax.broadcasted_iota(jnp.int32, (c, c), 0)
    col = lax.broadcasted_iota(jnp.int32, (c, c), 1)
    keep = (col >= row) if reverse else (col <= row)
    tri = jnp.where(keep, 1.0, 0.0).astype(bf16)

    z_hi, z_lo = _split_bf16(z_ref[...])
    w_hi, w_lo = _split_bf16(wg)
    pre = _dot(z_hi, w_hi) + _dot(z_lo, w_hi) + _dot(z_hi, w_lo) + bg

    order = range(nchunks - 1, -1, -1) if reverse else range(nchunks)
    for ci in order:
        rows = slice(ci * c, (ci + 1) * c)
        for h in range(GLA_HEADS):
            kc = slice(h * GLA_DK, (h + 1) * GLA_DK)
            vc = slice(h * GLA_DV, (h + 1) * GLA_DV)
            t = pre[rows, kc]
            la = -(jnp.maximum(-t, 0.0) + jnp.log(1.0 + jnp.exp(-jnp.abs(t)))) / GLA_GATE_TEMP
            la_hi, la_lo = _split_bf16(la)
            bc = _dot(tri, la_hi) + _dot(tri, la_lo)
            if reverse:
                b_end = bc[0:1, :]
                m = bc[c // 2:c // 2 + 1, :]
            else:
                b_end = bc[c - 1:c, :]
                m = bc[c // 2 - 1:c // 2, :]
            e_q = jnp.exp(bc - m)
            e_k = jnp.exp(m - bc)
            qe = q_ref[rows, kc].astype(f32) * e_q
            ke = k_ref[rows, kc].astype(f32) * e_k
            q_full = qe * jnp.exp(m)
            k_end = ke * jnp.exp(b_end - m)
            dec = jnp.exp(b_end)

            att = lax.dot_general(qe.astype(bf16), ke.astype(bf16), (((1,), (1,)), ((), ())),
                                  preferred_element_type=f32)
            att = jnp.where(keep, att, 0.0)
            v_c = v_ref[rows, vc]
            s_old = s_ref[h]
            lhs = jnp.concatenate([att.astype(bf16), q_full.astype(bf16)], axis=1)
            rhs = jnp.concatenate([v_c, s_old.astype(bf16)], axis=0)
            o_ref[rows, vc] = _dot(lhs, rhs).astype(o_ref.dtype)

            kv = _dot(k_end.T.astype(bf16), v_c)
            dec_col = jnp.broadcast_to(dec, (GLA_DK, GLA_DK)).T
            s_ref[h] = s_old * jnp.concatenate([dec_col, dec_col], axis=1) + kv


def _gla_kernel(qf_ref, kf_ref, vf_ref, zf_ref, qb_ref, kb_ref, vb_ref, zb_ref, wg_ref, bg_ref,
                of_ref, ob_ref, sf_ref, sb_ref, *, nchunks):
    @pl.when(pl.program_id(1) == 0)
    def _():
        sf_ref[...] = jnp.zeros_like(sf_ref)
        sb_ref[...] = jnp.zeros_like(sb_ref)

    _gla_direction(qf_ref, kf_ref, vf_ref, zf_ref, wg_ref[:, :QK_W], bg_ref[:, :QK_W],
                   of_ref, sf_ref, reverse=False, nchunks=nchunks)
    _gla_direction(qb_ref, kb_ref, vb_ref, zb_ref, wg_ref[:, QK_W:], bg_ref[:, QK_W:],
                   ob_ref, sb_ref, reverse=True, nchunks=nchunks)


def _gla(q, k, v, z, wg, bg, *, layer, nb, nbc, nbl):
    n = q.shape[0]
    gb = GLA_BLOCK

    def fwd_blk(b, s):
        return jnp.where(s < nbc, b * nbc + s, nb * nbc + b * nbl + (s - nbc))

    def bwd_blk(b, s):
        return jnp.where(s < nbc, b * nbc + (nbc - 1 - s), nb * nbc + b * nbl + (nbl - 1 - (s - nbc)))

    def spec(w, blk):
        return pl.BlockSpec((gb, w), lambda b, s: (blk(b, s), 0))

    ins = []
    for blk in (fwd_blk, bwd_blk):
        ins += [spec(QK_W, blk), spec(QK_W, blk), spec(V_W, blk), spec(LANES, blk)]
    ins += [pl.BlockSpec((None, LANES, 2 * QK_W), lambda b, s: (layer, 0, 0)),
            pl.BlockSpec((None, 1, 2 * QK_W), lambda b, s: (layer, 0, 0))]
    return pl.pallas_call(
        functools.partial(_gla_kernel, nchunks=gb // GLA_CHUNK),
        grid=(nb, nbc + nbl),
        in_specs=ins,
        out_specs=[spec(V_W, fwd_blk), spec(V_W, bwd_blk)],
        out_shape=(jax.ShapeDtypeStruct((n, V_W), bf16), jax.ShapeDtypeStruct((n, V_W), bf16)),
        scratch_shapes=[pltpu.VMEM((GLA_HEADS, GLA_DK, GLA_DV), f32),
                        pltpu.VMEM((GLA_HEADS, GLA_DK, GLA_DV), f32)],
        compiler_params=pltpu.CompilerParams(
            dimension_semantics=("arbitrary", "arbitrary"), vmem_limit_bytes=VMEM_LIMIT),
        name="gla",
    )(q, k, v, z, q, k, v, z, wg, bg)


def _mixer_kernel(x_ref, of_ref, ob_ref, rs_ref, pb_ref, sgp_ref, sgg_ref, g1_ref, gnw_ref,
                  wbg_ref, wbp_ref, wout_ref, lnw_ref, lnb_ref, out_ref, *, alpha):
    o = of_ref[...].astype(f32) + ob_ref[...].astype(f32)
    heads = []
    for h in range(GLA_HEADS):
        oh = o[:, h * GLA_DV:(h + 1) * GLA_DV]
        ms = jnp.mean(oh * oh, axis=-1, keepdims=True)
        heads.append(oh * lax.rsqrt(ms + LN_EPS))
    of = jnp.concatenate(heads, axis=1) * gnw_ref[...]
    y_gla = _dot((of * rs_ref[...].astype(f32)).astype(bf16), wbg_ref[...])
    y_pool = _dot(pb_ref[...], wbp_ref[...])
    merged = sgp_ref[...].astype(f32) * y_pool + sgg_ref[...].astype(f32) * y_gla
    mix = _dot(merged.astype(bf16), wout_ref[...])
    t = alpha * x_ref[...] + g1_ref[0] * mix
    out_ref[...] = _layer_norm(t) * lnw_ref[...] + lnb_ref[...]


def _mixer(xs, o_f, o_b, rs, pb, sgp, sgg, mod, gnw, wbg, wbp, wout, lnw, lnb, *,
           layer, tm, nct, tpb, nb, mod_rows, off, alpha):
    n, d = xs.shape
    nt = n // tm - off

    def mod_idx(i):
        j = i + off
        row = jnp.where(j < nct, nb, jnp.maximum(j - nct, 0) // tpb)
        return ((layer * 6 + 2) * mod_rows + row, 0, 0)

    tok = lambda w: pl.BlockSpec((tm, w), lambda i: (i + off, 0))
    const1 = pl.Buffered(1)
    vec = lambda w: pl.BlockSpec((None, 1, w), lambda i: (layer, 0, 0))
    mat = lambda a, b: pl.BlockSpec((None, a, b), lambda i: (layer, 0, 0), pipeline_mode=const1)
    return pl.pallas_call(
        functools.partial(_mixer_kernel, alpha=alpha),
        grid=(nt,),
        in_specs=[tok(d), tok(V_W), tok(V_W), tok(V_W), tok(POOL_WIDTH), tok(d), tok(d),
                  pl.BlockSpec((1, 1, d), mod_idx), vec(V_W),
                  mat(V_W, d), mat(POOL_WIDTH, d), mat(d, d), vec(d), vec(d)],
        out_specs=pl.BlockSpec((tm, d), lambda i: (i, 0)),
        out_shape=jax.ShapeDtypeStruct((nt * tm, d), f32),
        compiler_params=pltpu.CompilerParams(
            dimension_semantics=("arbitrary",), vmem_limit_bytes=VMEM_LIMIT),
        name="mixer",
    )(xs, o_f, o_b, rs, pb, sgp, sgg, mod, gnw, wbg, wbp, wout, lnw, lnb)


def _ffn_kernel(x_ref, sh_ref, sc_ref, g2_ref, wup_ref, cw_ref, cb_ref, wdn_ref, lnw_ref, lnb_ref,
                out_ref, *, alpha, tm, nct, off, ctx_len):
    x = x_ref[...]
    u = (x * (1.0 + sc_ref[0]) + sh_ref[0]).astype(bf16)

    is_ctx = (pl.program_id(0) + off) < nct
    row_len = jnp.where(is_ctx, ctx_len, GRID_W)
    pos = lax.broadcasted_iota(jnp.int32, (tm, 1), 0) & (row_len - 1)
    has_prev = jnp.where(pos != 0, 1.0, 0.0)
    has_next = jnp.where(pos != row_len - 1, 1.0, 0.0)

    acc = jnp.zeros((tm, D_MODEL), f32)
    for c in range(FFN_HIDDEN // FFN_CHUNK):
        cs = slice(c * FFN_CHUNK, (c + 1) * FFN_CHUNK)
        a = _dot(u, wup_ref[:, cs])
        gate = _dot(u, wup_ref[:, FFN_HIDDEN + c * FFN_CHUNK:FFN_HIDDEN + (c + 1) * FFN_CHUNK])
        a_prev = pltpu.roll(a, 1, 0) * has_prev
        a_next = pltpu.roll(a, tm - 1, 0) * has_next
        a = cw_ref[0:1, cs] * a_prev + cw_ref[1:2, cs] * a + cw_ref[2:3, cs] * a_next + cb_ref[:, cs]
        hid = 0.5 * a * (1.0 + lax.erf(a * (2.0 ** -0.5))) * gate
        acc = acc + _dot(hid.astype(bf16), wdn_ref[cs, :])
    t = alpha * x + g2_ref[0] * acc
    out_ref[...] = _layer_norm(t) * lnw_ref[...] + lnb_ref[...]


def _ffn(xs, mod, wup, cw, cb, wdn, lnw, lnb, *, layer, tm, nct, tpb, nb, mod_rows, off, alpha, ctx_len):
    n, d = xs.shape
    nt = n // tm

    def mod_idx(slot):
        def idx(i):
            j = i + off
            row = jnp.where(j < nct, nb, jnp.maximum(j - nct, 0) // tpb)
            return ((layer * 6 + slot) * mod_rows + row, 0, 0)
        return idx

    const1 = pl.Buffered(1)
    vec = lambda w: pl.BlockSpec((None, 1, w), lambda i: (layer, 0, 0))
    return pl.pallas_call(
        functools.partial(_ffn_kernel, alpha=alpha, tm=tm, nct=nct, off=off, ctx_len=ctx_len),
        grid=(nt,),
        in_specs=[pl.BlockSpec((tm, d), lambda i: (i, 0)),
                  pl.BlockSpec((1, 1, d), mod_idx(3)),
                  pl.BlockSpec((1, 1, d), mod_idx(4)),
                  pl.BlockSpec((1, 1, d), mod_idx(5)),
                  pl.BlockSpec((None, d, 2 * FFN_HIDDEN), lambda i: (layer, 0, 0), pipeline_mode=const1),
                  pl.BlockSpec((None, 3, FFN_HIDDEN), lambda i: (layer, 0, 0)),
                  vec(FFN_HIDDEN),
                  pl.BlockSpec((None, FFN_HIDDEN, d), lambda i: (layer, 0, 0), pipeline_mode=const1),
                  vec(d), vec(d)],
        out_specs=pl.BlockSpec((tm, d), lambda i: (i, 0)),
        out_shape=jax.ShapeDtypeStruct((n, d), f32),
        compiler_params=pltpu.CompilerParams(
            dimension_semantics=("arbitrary",), vmem_limit_bytes=VMEM_LIMIT),
        name="ffn",
    )(xs, mod, mod, mod, wup, cw, cb, wdn, lnw, lnb)


def _pos_embed_2d(rows, cols, dim):
    quarter = dim // 4
    omega = 1.0 / (10000.0 ** (jnp.arange(quarter, dtype=f32) / quarter))
    r = jnp.arange(rows, dtype=f32)[:, None] * omega
    cl = jnp.arange(cols, dtype=f32)[:, None] * omega
    er = jnp.concatenate([jnp.sin(r), jnp.cos(r)], -1)
    ec = jnp.concatenate([jnp.sin(cl), jnp.cos(cl)], -1)
    emb = jnp.concatenate([jnp.broadcast_to(er[:, None, :], (rows, cols, dim // 2)),
                           jnp.broadcast_to(ec[None, :, :], (rows, cols, dim // 2))], -1)
    return emb.reshape(rows * cols, dim)


def _pool_tables(ctx_len):
    t = np.arange(POOL_BLOCK)
    mats = np.zeros((2, POOL_GROUPS, POOL_BLOCK, POOL_BLOCK), np.float32)
    cnts = np.zeros((2, POOL_BLOCK, POOL_WIDTH), np.float32)
    for kind, row_len in enumerate((GRID_W, ctx_len)):
        base = (t // row_len) * row_len
        r = t - base
        for g, w in enumerate(POOL_WINDOWS):
            lo = np.clip(r - w // 2, 0, row_len) + base
            hi = np.clip(r + w // 2, 0, row_len) + base
            mats[kind, g] = (t[None, :] >= lo[:, None]) & (t[None, :] < hi[:, None])
            cnts[kind, :, g * POOL_GS:(g + 1) * POOL_GS] = (hi - lo)[:, None]
    return jnp.asarray(mats, bf16), jnp.asarray(cnts, f32)


def kernel(x, c, ctx, c_ctx, w_mod, b_mod, w_in, w_gate_f, b_gate_f, w_gate_b, b_gate_b, gla_norm_w, w_pool, pool_scale, w_br_pool, w_br_gla, w_out, ln1_w, ln1_b, w_up, conv_w, conv_b, w_down, ln2_w, ln2_b):
    nb, seq, d = x.shape
    ctx_len = ctx.shape[1]
    depth = w_mod.shape[0]
    tm = TOKEN_TILE
    assert d == D_MODEL and seq % GRID_W == 0
    assert ctx_len == POOL_BLOCK and ctx_len % GLA_BLOCK == 0 and seq % GLA_BLOCK == 0
    assert (nb * ctx_len) % tm == 0 and seq % tm == 0 and tm % POOL_BLOCK == 0
    alpha = (2.0 * depth) ** 0.25
    nct = nb * ctx_len // tm
    tpb = seq // tm
    nbc, nbl = ctx_len // GLA_BLOCK, seq // GLA_BLOCK

    mod_rows = -(-(nb + 1) // 8) * 8
    cm = jnp.zeros((mod_rows, d), f32).at[:nb].set(c).at[nb].set(c_ctx)
    mod = _modulation(cm, w_mod, b_mod).reshape(depth * 6 * mod_rows, 1, d)

    s0, s1, s2, s3, s4 = 512, 1024, 1536, 2560, 3584
    zcols = w_in[:, :, s4:s4 + 2 * GLA_GATE_RANK]
    w_in_p = jnp.concatenate(
        [w_in[:, :, :s4], w_in[:, :, s4 + 2 * GLA_GATE_RANK:],
         zcols, jnp.zeros((depth, d, LANES - 2 * GLA_GATE_RANK), f32)], axis=-1).astype(bf16)
    wg = jnp.zeros((depth, LANES, 2 * QK_W), f32)
    wg = wg.at[:, :GLA_GATE_RANK, :QK_W].set(w_gate_f)
    wg = wg.at[:, GLA_GATE_RANK:2 * GLA_GATE_RANK, QK_W:].set(w_gate_b)
    bg = jnp.concatenate([b_gate_f, b_gate_b], axis=-1).reshape(depth, 1, 2 * QK_W)
    w_pool_b = w_pool.astype(bf16)
    wbp, wbg, wo = w_br_pool.astype(bf16), w_br_gla.astype(bf16), w_out.astype(bf16)
    wup, wdn = w_up.astype(bf16), w_down.astype(bf16)
    row3 = lambda a: a.reshape(depth, 1, a.shape[-1])
    pmats, cnts = _pool_tables(ctx_len)

    pos = _pos_embed_2d(seq // GRID_W, GRID_W, d)
    xs = _embed(ctx.reshape(nb * ctx_len, d), x.reshape(nb * seq, d), pos, tm)

    common = dict(tm=tm, nct=nct, tpb=tpb, nb=nb, mod_rows=mod_rows)
    for l in range(depth):
        off = nct if l == depth - 1 else 0
        pb, q, k, v, rs, sgp, sgg, z = _in_proj(xs, mod, w_in_p, pmats, cnts, w_pool_b, row3(pool_scale),
                                                  layer=l, **common)
        o_f, o_b = _gla(q, k, v, z, wg, bg, layer=l, nb=nb, nbc=nbc, nbl=nbl)
        xs = _mixer(xs, o_f, o_b, rs, pb, sgp, sgg, mod, row3(gla_norm_w), wbg, wbp, wo,
                    row3(ln1_w), row3(ln1_b), layer=l, off=off, alpha=alpha, **common)
        xs = _ffn(xs, mod, wup, conv_w, row3(conv_b), wdn, row3(ln2_w), row3(ln2_b),
                  layer=l, off=off, alpha=alpha, ctx_len=ctx_len, **common)
    return xs.reshape(nb, seq, d)
```

```python
import functools

import numpy as np
import jax
import jax.numpy as jnp
from jax import lax
from jax.experimental import pallas as pl
from jax.experimental.pallas import tpu as pltpu

f32 = jnp.float32
bf16 = jnp.bfloat16

D_MODEL = 1024
GRID_W = 64
POOL_WIDTH = 512
POOL_GROUPS = 4
POOL_GS = POOL_WIDTH // POOL_GROUPS
POOL_WINDOWS = (2, 4, 8, 16)
GLA_HEADS = 4
GLA_DK = 128
GLA_DV = 256
QK_W = GLA_HEADS * GLA_DK
V_W = GLA_HEADS * GLA_DV
GLA_GATE_RANK = 16
GLA_GATE_TEMP = 16.0
FFN_HIDDEN = 2816
LN_EPS = 1e-6

LANES = 128
POOL_BLOCK = 256
GLA_CHUNK = 128
GLA_BLOCK = 256
FFN_CHUNK = 256
TOKEN_TILE = 512
VMEM_LIMIT = 56 * 1024 * 1024

C_POOL, C_Q, C_K, C_V, C_R, C_GP, C_GG, C_Z = 0, 512, 1024, 1536, 2560, 3584, 4608, 5632
N_PACKED = C_Z + LANES
Z_COPY = 2 * GLA_GATE_RANK
LOG2E = 1.4426950408889634


def _sigmoid(t):
    return 1.0 / (1.0 + jnp.exp(-t))


def _layer_norm(t):
    mu = jnp.mean(t, axis=-1, keepdims=True)
    d = t - mu
    var = jnp.mean(d * d, axis=-1, keepdims=True)
    return d * lax.rsqrt(var + LN_EPS)


def _split_bf16(t):
    hi = t.astype(bf16)
    lo = (t - hi.astype(f32)).astype(bf16)
    return hi, lo


def _dot(a, b):
    return jnp.dot(a, b, preferred_element_type=f32)


def _mod_kernel(c_ref, w_ref, b_ref, o_ref):
    cs = c_ref[...]
    s = cs * _sigmoid(cs)
    o_ref[0, 0] = jnp.dot(s, w_ref[0], preferred_element_type=f32,
                          precision=lax.Precision.HIGHEST) + b_ref[0]


def _modulation(cm, w_mod, b_mod):
    depth, d, _ = w_mod.shape
    rows = cm.shape[0]
    return pl.pallas_call(
        _mod_kernel,
        grid=(depth, 6),
        in_specs=[
            pl.BlockSpec((rows, d), lambda l, s: (0, 0)),
            pl.BlockSpec((1, d, d), lambda l, s: (l, 0, s)),
            pl.BlockSpec((1, 1, d), lambda l, s: (l, 0, s)),
        ],
        out_specs=pl.BlockSpec((1, 1, rows, d), lambda l, s: (l, s, 0, 0)),
        out_shape=jax.ShapeDtypeStruct((depth, 6, rows, d), f32),
        compiler_params=pltpu.CompilerParams(
            dimension_semantics=("arbitrary", "arbitrary"), vmem_limit_bytes=VMEM_LIMIT),
        name="modulation",
    )(cm, w_mod, b_mod.reshape(depth, 1, 6 * d))


def _embed_kernel(ctx_ref, x_ref, pos_ref, o_ref, *, nct):
    i = pl.program_id(0)

    @pl.when(i < nct)
    def _():
        o_ref[...] = _layer_norm(ctx_ref[...])

    @pl.when(i >= nct)
    def _():
        o_ref[...] = _layer_norm(x_ref[...] + pos_ref[...])


def _embed(ctx_flat, x_flat, pos, tm):
    d = x_flat.shape[1]
    nct = ctx_flat.shape[0] // tm
    nlt = x_flat.shape[0] // tm
    npos = pos.shape[0] // tm
    return pl.pallas_call(
        functools.partial(_embed_kernel, nct=nct),
        grid=(nct + nlt,),
        in_specs=[
            pl.BlockSpec((tm, d), lambda i: (jnp.minimum(i, nct - 1), 0)),
            pl.BlockSpec((tm, d), lambda i: (jnp.maximum(i - nct, 0), 0)),
            pl.BlockSpec((tm, d), lambda i: (jnp.maximum(i - nct, 0) % npos, 0)),
        ],
        out_specs=pl.BlockSpec((tm, d), lambda i: (i, 0)),
        out_shape=jax.ShapeDtypeStruct(((nct + nlt) * tm, d), f32),
        compiler_params=pltpu.CompilerParams(
            dimension_semantics=("arbitrary",), vmem_limit_bytes=VMEM_LIMIT),
        name="embed",
    )(ctx_flat, x_flat, pos)


def _in_proj_kernel(x_ref, sh_ref, sc_ref, w_ref, pm_ref, cnt_ref, wpool_ref, pscale_ref,
                    pb_ref, q_ref, k_ref, v_ref, rs_ref, sgp_ref, sgg_ref, z_ref, *, tm):
    u = (x_ref[...] * (1.0 + sc_ref[0]) + sh_ref[0]).astype(bf16)

    def proj(a, n):
        return _dot(u, w_ref[:, a:a + n])

    p = proj(C_POOL, POOL_WIDTH)
    for sb in range(tm // POOL_BLOCK):
        rows = slice(sb * POOL_BLOCK, (sb + 1) * POOL_BLOCK)
        for g in range(POOL_GROUPS):
            cols = slice(g * POOL_GS, (g + 1) * POOL_GS)
            pg = p[rows, cols]
            hi, lo = _split_bf16(pg)
            band = pm_ref[0, g]
            s = _dot(band, hi) + _dot(band, lo)
            y = s / cnt_ref[0, :, cols] - pg
            yw = _dot(y.astype(bf16), wpool_ref[g]) * pscale_ref[:, cols]
            pb_ref[rows, cols] = yw.astype(bf16)

    half = 512
    q_ref[...] = (proj(C_Q, QK_W) * (GLA_DK ** -0.5)).astype(bf16)
    k_ref[...] = proj(C_K, QK_W).astype(bf16)
    for j in range(V_W // half):
        cs = slice(j * half, (j + 1) * half)
        v_ref[:, cs] = proj(C_V + j * half, half).astype(bf16)
        r = proj(C_R + j * half, half)
        rs_ref[:, cs] = (r * _sigmoid(r)).astype(bf16)
        sgp_ref[:, cs] = _sigmoid(proj(C_GP + j * half, half)).astype(bf16)
        sgg_ref[:, cs] = _sigmoid(proj(C_GG + j * half, half)).astype(bf16)
    z3 = proj(C_Z, LANES)
    z_hi, z_lo = _split_bf16(z3)
    lane = lax.broadcasted_iota(jnp.int32, z3.shape, 1)
    in_lo = (lane >= Z_COPY) & (lane < 2 * Z_COPY)
    z_ref[...] = jnp.where(in_lo, z_lo, z_hi)


def _in_proj(xs, mod, w_in_p, pmats, cnts, w_pool_b, pool_scale, *, layer, tm, nct, tpb, nb, mod_rows):
    n, d = xs.shape
    nt = n // tm

    def mod_idx(slot):
        def idx(i):
            row = jnp.where(i < nct, nb, jnp.maximum(i - nct, 0) // tpb)
            return ((layer * 6 + slot) * mod_rows + row, 0, 0)
        return idx

    def kind(i):
        return jnp.where(i < nct, 1, 0)

    tok = lambda w: pl.BlockSpec((tm, w), lambda i: (i, 0))
    const1 = pl.Buffered(1)
    out_shapes = (
        jax.ShapeDtypeStruct((n, POOL_WIDTH), bf16),
        jax.ShapeDtypeStruct((n, QK_W), bf16),
        jax.ShapeDtypeStruct((n, QK_W), bf16),
        jax.ShapeDtypeStruct((n, V_W), bf16),
        jax.ShapeDtypeStruct((n, V_W), bf16),
        jax.ShapeDtypeStruct((n, D_MODEL), bf16),
        jax.ShapeDtypeStruct((n, D_MODEL), bf16),
        jax.ShapeDtypeStruct((n, LANES), bf16),
    )
    return pl.pallas_call(
        functools.partial(_in_proj_kernel, tm=tm),
        grid=(nt,),
        in_specs=[
            tok(d),
            pl.BlockSpec((1, 1, d), mod_idx(0)),
            pl.BlockSpec((1, 1, d), mod_idx(1)),
            pl.BlockSpec((None, d, N_PACKED), lambda i: (layer, 0, 0), pipeline_mode=const1),
            pl.BlockSpec((1, POOL_GROUPS, POOL_BLOCK, POOL_BLOCK), lambda i: (kind(i), 0, 0, 0)),
            pl.BlockSpec((1, POOL_BLOCK, POOL_WIDTH), lambda i: (kind(i), 0, 0)),
            pl.BlockSpec((None, POOL_GROUPS, POOL_GS, POOL_GS), lambda i: (layer, 0, 0, 0)),
            pl.BlockSpec((None, 1, POOL_WIDTH), lambda i: (layer, 0, 0)),
        ],
        out_specs=[tok(POOL_WIDTH), tok(QK_W), tok(QK_W), tok(V_W), tok(V_W),
                   tok(D_MODEL), tok(D_MODEL), tok(LANES)],
        out_shape=out_shapes,
        compiler_params=pltpu.CompilerParams(
            dimension_semantics=("arbitrary",), vmem_limit_bytes=VMEM_LIMIT),
        name="in_proj",
    )(xs, mod, mod, w_in_p, pmats, cnts, w_pool_b, pool_scale)


def _gla_direction(q_ref, k_ref, v_ref, z_ref, wg, bg, o_ref, s_ref, *, reverse, nchunks):
    c = GLA_CHUNK
    row = lax.broadcasted_iota(jnp.int32, (c, c), 0)
    col = lax.broadcasted_iota(jnp.int32, (c, c), 1)
    keep = (col >= row) if reverse else (col <= row)
    tri = jnp.where(keep, 1.0, 0.0).astype(bf16)
    tri2 = jnp.concatenate([tri, tri], axis=1)

    pre = _dot(z_ref[...], wg) + bg
    soft = jnp.maximum(-pre, 0.0) + jnp.log(1.0 + jnp.exp(-jnp.abs(pre)))
    la_hi, la_lo = _split_bf16(soft * (-LOG2E / GLA_GATE_TEMP))

    chunks = []
    ends = []
    for ci in range(nchunks):
        rows = slice(ci * c, (ci + 1) * c)
        bc = _dot(tri2, jnp.concatenate([la_hi[rows], la_lo[rows]], axis=0))
        if reverse:
            b_end = bc[0:1, :]
            m = bc[c // 2:c // 2 + 1, :]
        else:
            b_end = bc[c - 1:c, :]
            m = bc[c // 2 - 1:c // 2, :]
        rel = bc - m
        qe = q_ref[rows, :].astype(f32) * jnp.exp2(rel)
        ke = k_ref[rows, :].astype(f32) * jnp.exp2(-rel)
        q_full = qe * jnp.exp2(m)
        k_end = ke * jnp.exp2(b_end - m)
        chunks.append((qe.astype(bf16), ke.astype(bf16), q_full.astype(bf16), k_end))
        ends += [b_end[:, h * GLA_DK:(h + 1) * GLA_DK] for h in range(GLA_HEADS)]

    n_units = nchunks * GLA_HEADS
    end_rows = jnp.concatenate(ends + [jnp.zeros((GLA_DK - n_units, GLA_DK), f32)], axis=0)
    dec_cols = jnp.exp2(end_rows.T)

    order = range(nchunks - 1, -1, -1) if reverse else range(nchunks)
    zero_blk = jnp.zeros((GLA_DK, c), bf16)
    for h in range(GLA_HEADS):
        kc = slice(h * GLA_DK, (h + 1) * GLA_DK)
        vc = slice(h * GLA_DV, (h + 1) * GLA_DV)
        s = s_ref[h]
        for ci in order:
            rows = slice(ci * c, (ci + 1) * c)
            qe, ke, q_full, k_end = chunks[ci]
            att = lax.dot_general(qe[:, kc], ke[:, kc], (((1,), (1,)), ((), ())),
                                  preferred_element_type=f32)
            att = jnp.where(keep, att, 0.0).astype(bf16)
            lhs = jnp.concatenate(
                [jnp.concatenate([att, q_full[:, kc]], axis=1),
                 jnp.concatenate([k_end[:, kc].T.astype(bf16), zero_blk], axis=1)], axis=0)
            rhs = jnp.concatenate([v_ref[rows, vc], s.astype(bf16)], axis=0)
            res = _dot(lhs, rhs)
            o_ref[rows, vc] = res[:c].astype(o_ref.dtype)
            u = ci * GLA_HEADS + h
            s = s * dec_cols[:, u:u + 1] + res[c:]
        s_ref[h] = s


def _gla_kernel(qf_ref, kf_ref, vf_ref, zf_ref, qb_ref, kb_ref, vb_ref, zb_ref, wg_ref, bg_ref,
                of_ref, ob_ref, sf_ref, sb_ref, *, nchunks):
    @pl.when(pl.program_id(1) == 0)
    def _():
        sf_ref[...] = jnp.zeros_like(sf_ref)
        sb_ref[...] = jnp.zeros_like(sb_ref)

    _gla_direction(qf_ref, kf_ref, vf_ref, zf_ref, wg_ref[:, :QK_W], bg_ref[:, :QK_W],
                   of_ref, sf_ref, reverse=False, nchunks=nchunks)
    _gla_direction(qb_ref, kb_ref, vb_ref, zb_ref, wg_ref[:, QK_W:], bg_ref[:, QK_W:],
                   ob_ref, sb_ref, reverse=True, nchunks=nchunks)


def _gla(q, k, v, z, wg, bg, *, layer, nb, nbc, nbl):
    n = q.shape[0]
    gb = GLA_BLOCK

    def fwd_blk(b, s):
        return jnp.where(s < nbc, b * nbc + s, nb * nbc + b * nbl + (s - nbc))

    def bwd_blk(b, s):
        return jnp.where(s < nbc, b * nbc + (nbc - 1 - s), nb * nbc + b * nbl + (nbl - 1 - (s - nbc)))

    def spec(w, blk):
        return pl.BlockSpec((gb, w), lambda b, s: (blk(b, s), 0))

    ins = []
    for blk in (fwd_blk, bwd_blk):
        ins += [spec(QK_W, blk), spec(QK_W, blk), spec(V_W, blk), spec(LANES, blk)]
    ins += [pl.BlockSpec((None, LANES, 2 * QK_W), lambda b, s: (layer, 0, 0)),
            pl.BlockSpec((None, 1, 2 * QK_W), lambda b, s: (layer, 0, 0))]
    return pl.pallas_call(
        functools.partial(_gla_kernel, nchunks=gb // GLA_CHUNK),
        grid=(nb, nbc + nbl),
        in_specs=ins,
        out_specs=[spec(V_W, fwd_blk), spec(V_W, bwd_blk)],
        out_shape=(jax.ShapeDtypeStruct((n, V_W), bf16), jax.ShapeDtypeStruct((n, V_W), bf16)),
        scratch_shapes=[pltpu.VMEM((GLA_HEADS, GLA_DK, GLA_DV), f32),
                        pltpu.VMEM((GLA_HEADS, GLA_DK, GLA_DV), f32)],
        compiler_params=pltpu.CompilerParams(
            dimension_semantics=("arbitrary", "arbitrary"), vmem_limit_bytes=VMEM_LIMIT),
        name="gla",
    )(q, k, v, z, q, k, v, z, wg, bg)


def _mixer_kernel(x_ref, of_ref, ob_ref, rs_ref, pb_ref, sgp_ref, sgg_ref, g1_ref, gnw_ref,
                  wbg_ref, wbp_ref, wout_ref, lnw_ref, lnb_ref, out_ref, *, alpha):
    o = of_ref[...].astype(f32) + ob_ref[...].astype(f32)
    heads = []
    for h in range(GLA_HEADS):
        oh = o[:, h * GLA_DV:(h + 1) * GLA_DV]
        ms = jnp.mean(oh * oh, axis=-1, keepdims=True)
        heads.append(oh * lax.rsqrt(ms + LN_EPS))
    of = jnp.concatenate(heads, axis=1) * gnw_ref[...]
    y_gla = _dot((of * rs_ref[...].astype(f32)).astype(bf16), wbg_ref[...])
    y_pool = _dot(pb_ref[...], wbp_ref[...])
    merged = sgp_ref[...].astype(f32) * y_pool + sgg_ref[...].astype(f32) * y_gla
    mix = _dot(merged.astype(bf16), wout_ref[...])
    t = alpha * x_ref[...] + g1_ref[0] * mix
    out_ref[...] = _layer_norm(t) * lnw_ref[...] + lnb_ref[...]


def _mixer(xs, o_f, o_b, rs, pb, sgp, sgg, mod, gnw, wbg, wbp, wout, lnw, lnb, *,
           layer, tm, nct, tpb, nb, mod_rows, off, alpha):
    n, d = xs.shape
    nt = n // tm - off

    def mod_idx(i):
        j = i + off
        row = jnp.where(j < nct, nb, jnp.maximum(j - nct, 0) // tpb)
        return ((layer * 6 + 2) * mod_rows + row, 0, 0)

    tok = lambda w: pl.BlockSpec((tm, w), lambda i: (i + off, 0))
    const1 = pl.Buffered(1)
    vec = lambda w: pl.BlockSpec((None, 1, w), lambda i: (layer, 0, 0))
    mat = lambda a, b: pl.BlockSpec((None, a, b), lambda i: (layer, 0, 0), pipeline_mode=const1)
    return pl.pallas_call(
        functools.partial(_mixer_kernel, alpha=alpha),
        grid=(nt,),
        in_specs=[tok(d), tok(V_W), tok(V_W), tok(V_W), tok(POOL_WIDTH), tok(d), tok(d),
                  pl.BlockSpec((1, 1, d), mod_idx), vec(V_W),
                  mat(V_W, d), mat(POOL_WIDTH, d), mat(d, d), vec(d), vec(d)],
        out_specs=pl.BlockSpec((tm, d), lambda i: (i, 0)),
        out_shape=jax.ShapeDtypeStruct((nt * tm, d), f32),
        compiler_params=pltpu.CompilerParams(
            dimension_semantics=("arbitrary",), vmem_limit_bytes=VMEM_LIMIT),
        name="mixer",
    )(xs, o_f, o_b, rs, pb, sgp, sgg, mod, gnw, wbg, wbp, wout, lnw, lnb)


def _ffn_kernel(x_ref, sh_ref, sc_ref, g2_ref, wup_ref, cw_ref, cb_ref, wdn_ref, lnw_ref, lnb_ref,
                out_ref, hid_ref, *, alpha, tm, nct, off, ctx_len):
    x = x_ref[...]
    u = (x * (1.0 + sc_ref[0]) + sh_ref[0]).astype(bf16)

    is_ctx = (pl.program_id(0) + off) < nct
    row_len = jnp.where(is_ctx, ctx_len, GRID_W)
    pos = lax.broadcasted_iota(jnp.int32, (tm, 1), 0) & (row_len - 1)
    has_prev = jnp.where(pos != 0, 1.0, 0.0)
    has_next = jnp.where(pos != row_len - 1, 1.0, 0.0)

    for c in range(FFN_HIDDEN // FFN_CHUNK):
        cs = slice(c * FFN_CHUNK, (c + 1) * FFN_CHUNK)
        a = _dot(u, wup_ref[:, cs])
        gate = _dot(u, wup_ref[:, FFN_HIDDEN + c * FFN_CHUNK:FFN_HIDDEN + (c + 1) * FFN_CHUNK])
        a_prev = pltpu.roll(a, 1, 0) * has_prev
        a_next = pltpu.roll(a, tm - 1, 0) * has_next
        a = cw_ref[0:1, cs] * a_prev + cw_ref[1:2, cs] * a + cw_ref[2:3, cs] * a_next + cb_ref[:, cs]
        hid = 0.5 * a * (1.0 + lax.erf(a * (2.0 ** -0.5))) * gate
        hid_ref[:, cs] = hid.astype(bf16)
    t = alpha * x + g2_ref[0] * _dot(hid_ref[...], wdn_ref[...])
    out_ref[...] = _layer_norm(t) * lnw_ref[...] + lnb_ref[...]


def _ffn(xs, mod, wup, cw, cb, wdn, lnw, lnb, *, layer, tm, nct, tpb, nb, mod_rows, off, alpha, ctx_len):
    n, d = xs.shape
    nt = n // tm

    def mod_idx(slot):
        def idx(i):
            j = i + off
            row = jnp.where(j < nct, nb, jnp.maximum(j - nct, 0) // tpb)
            return ((layer * 6 + slot) * mod_rows + row, 0, 0)
        return idx

    const1 = pl.Buffered(1)
    vec = lambda w: pl.BlockSpec((None, 1, w), lambda i: (layer, 0, 0))
    return pl.pallas_call(
        functools.partial(_ffn_kernel, alpha=alpha, tm=tm, nct=nct, off=off, ctx_len=ctx_len),
        grid=(nt,),
        in_specs=[pl.BlockSpec((tm, d), lambda i: (i, 0)),
                  pl.BlockSpec((1, 1, d), mod_idx(3)),
                  pl.BlockSpec((1, 1, d), mod_idx(4)),
                  pl.BlockSpec((1, 1, d), mod_idx(5)),
                  pl.BlockSpec((None, d, 2 * FFN_HIDDEN), lambda i: (layer, 0, 0), pipeline_mode=const1),
                  pl.BlockSpec((None, 3, FFN_HIDDEN), lambda i: (layer, 0, 0)),
                  vec(FFN_HIDDEN),
                  pl.BlockSpec((None, FFN_HIDDEN, d), lambda i: (layer, 0, 0), pipeline_mode=const1),
                  vec(d), vec(d)],
        out_specs=pl.BlockSpec((tm, d), lambda i: (i, 0)),
        out_shape=jax.ShapeDtypeStruct((n, d), f32),
        scratch_shapes=[pltpu.VMEM((tm, FFN_HIDDEN), bf16)],
        compiler_params=pltpu.CompilerParams(
            dimension_semantics=("arbitrary",), vmem_limit_bytes=VMEM_LIMIT),
        name="ffn",
    )(xs, mod, mod, mod, wup, cw, cb, wdn, lnw, lnb)


def _pos_embed_2d(rows, cols, dim):
    quarter = dim // 4
    omega = 1.0 / (10000.0 ** (jnp.arange(quarter, dtype=f32) / quarter))
    r = jnp.arange(rows, dtype=f32)[:, None] * omega
    cl = jnp.arange(cols, dtype=f32)[:, None] * omega
    er = jnp.concatenate([jnp.sin(r), jnp.cos(r)], -1)
    ec = jnp.concatenate([jnp.sin(cl), jnp.cos(cl)], -1)
    emb = jnp.concatenate([jnp.broadcast_to(er[:, None, :], (rows, cols, dim // 2)),
                           jnp.broadcast_to(ec[None, :, :], (rows, cols, dim // 2))], -1)
    return emb.reshape(rows * cols, dim)


def _pool_tables(ctx_len):
    t = np.arange(POOL_BLOCK)
    mats = np.zeros((2, POOL_GROUPS, POOL_BLOCK, POOL_BLOCK), np.float32)
    cnts = np.zeros((2, POOL_BLOCK, POOL_WIDTH), np.float32)
    for kind, row_len in enumerate((GRID_W, ctx_len)):
        base = (t // row_len) * row_len
        r = t - base
        for g, w in enumerate(POOL_WINDOWS):
            lo = np.clip(r - w // 2, 0, row_len) + base
            hi = np.clip(r + w // 2, 0, row_len) + base
            mats[kind, g] = (t[None, :] >= lo[:, None]) & (t[None, :] < hi[:, None])
            cnts[kind, :, g * POOL_GS:(g + 1) * POOL_GS] = (hi - lo)[:, None]
    return jnp.asarray(mats, bf16), jnp.asarray(cnts, f32)


def kernel(x, c, ctx, c_ctx, w_mod, b_mod, w_in, w_gate_f, b_gate_f, w_gate_b, b_gate_b, gla_norm_w, w_pool, pool_scale, w_br_pool, w_br_gla, w_out, ln1_w, ln1_b, w_up, conv_w, conv_b, w_down, ln2_w, ln2_b):
    nb, seq, d = x.shape
    ctx_len = ctx.shape[1]
    depth = w_mod.shape[0]
    tm = TOKEN_TILE
    assert d == D_MODEL and seq % GRID_W == 0
    assert ctx_len == POOL_BLOCK and ctx_len % GLA_BLOCK == 0 and seq % GLA_BLOCK == 0
    assert (nb * ctx_len) % tm == 0 and seq % tm == 0 and tm % POOL_BLOCK == 0
    alpha = (2.0 * depth) ** 0.25
    nct = nb * ctx_len // tm
    tpb = seq // tm
    nbc, nbl = ctx_len // GLA_BLOCK, seq // GLA_BLOCK

    mod_rows = -(-(nb + 1) // 8) * 8
    cm = jnp.zeros((mod_rows, d), f32).at[:nb].set(c).at[nb].set(c_ctx)
    mod = _modulation(cm, w_mod, b_mod).reshape(depth * 6 * mod_rows, 1, d)

    s4 = 3584
    zcols = w_in[:, :, s4:s4 + Z_COPY]
    w_in_p = jnp.concatenate(
        [w_in[:, :, :s4], w_in[:, :, s4 + Z_COPY:], zcols, zcols, zcols,
         jnp.zeros((depth, d, LANES - 3 * Z_COPY), f32)], axis=-1).astype(bf16)
    wg32 = jnp.zeros((depth, Z_COPY, 2 * QK_W), f32)
    wg32 = wg32.at[:, :GLA_GATE_RANK, :QK_W].set(w_gate_f)
    wg32 = wg32.at[:, GLA_GATE_RANK:, QK_W:].set(w_gate_b)
    wg_hi = wg32.astype(bf16)
    wg_lo = (wg32 - wg_hi.astype(f32)).astype(bf16)
    wg = jnp.concatenate([wg_hi, wg_hi, wg_lo,
                          jnp.zeros((depth, LANES - 3 * Z_COPY, 2 * QK_W), bf16)], axis=1)
    bg = jnp.concatenate([b_gate_f, b_gate_b], axis=-1).reshape(depth, 1, 2 * QK_W)
    w_pool_b = w_pool.astype(bf16)
    wbp, wbg, wo = w_br_pool.astype(bf16), w_br_gla.astype(bf16), w_out.astype(bf16)
    wup, wdn = w_up.astype(bf16), w_down.astype(bf16)
    row3 = lambda a: a.reshape(depth, 1, a.shape[-1])
    pmats, cnts = _pool_tables(ctx_len)

    pos = _pos_embed_2d(seq // GRID_W, GRID_W, d)
    xs = _embed(ctx.reshape(nb * ctx_len, d), x.reshape(nb * seq, d), pos, tm)

    common = dict(tm=tm, nct=nct, tpb=tpb, nb=nb, mod_rows=mod_rows)
    for l in range(depth):
        off = nct if l == depth - 1 else 0
        pb, q, k, v, rs, sgp, sgg, z = _in_proj(xs, mod, w_in_p, pmats, cnts, w_pool_b, row3(pool_scale),
                                                  layer=l, **common)
        o_f, o_b = _gla(q, k, v, z, wg, bg, layer=l, nb=nb, nbc=nbc, nbl=nbl)
        xs = _mixer(xs, o_f, o_b, rs, pb, sgp, sgg, mod, row3(gla_norm_w), wbg, wbp, wo,
                    row3(ln1_w), row3(ln1_b), layer=l, off=off, alpha=alpha, **common)
        xs = _ffn(xs, mod, wup, conv_w, row3(conv_b), wdn, row3(ln2_w), row3(ln2_b),
                  layer=l, off=off, alpha=alpha, ctx_len=ctx_len, **common)
    return xs.reshape(nb, seq, d)
```

```python
import functools

import numpy as np
import jax
import jax.numpy as jnp
from jax import lax
from jax.experimental import pallas as pl
from jax.experimental.pallas import tpu as pltpu

f32 = jnp.float32
bf16 = jnp.bfloat16

D_MODEL = 1024
GRID_W = 64
POOL_WIDTH = 512
POOL_GROUPS = 4
POOL_GS = POOL_WIDTH // POOL_GROUPS
POOL_WINDOWS = (2, 4, 8, 16)
GLA_HEADS = 4
GLA_DK = 128
GLA_DV = 256
QK_W = GLA_HEADS * GLA_DK
V_W = GLA_HEADS * GLA_DV
GLA_GATE_RANK = 16
GLA_GATE_TEMP = 16.0
FFN_HIDDEN = 2816
LN_EPS = 1e-6

LANES = 128
POOL_BLOCK = 256
GLA_CHUNK = 128
GLA_BLOCK = 256
FFN_CHUNK = 256
PROJ_CHUNK = 256
TOKEN_TILE = 512
VMEM_LIMIT = 56 * 1024 * 1024

C_POOL, C_Q, C_K, C_V, C_R, C_GP, C_GG, C_Z = 0, 512, 1024, 1536, 2560, 3584, 4608, 5632
N_PACKED = C_Z + LANES
Z_COPY = 2 * GLA_GATE_RANK
LOG2E = 1.4426950408889634


def _sigmoid(t):
    return 1.0 / (1.0 + jnp.exp(-t))


def _layer_norm(t):
    mu = jnp.mean(t, axis=-1, keepdims=True)
    d = t - mu
    var = jnp.mean(d * d, axis=-1, keepdims=True)
    return d * lax.rsqrt(var + LN_EPS)


def _split_bf16(t):
    hi = t.astype(bf16)
    lo = (t - hi.astype(f32)).astype(bf16)
    return hi, lo


def _dot(a, b):
    return jnp.dot(a, b, preferred_element_type=f32)


def _mod_kernel(c_ref, w_ref, b_ref, o_ref):
    cs = c_ref[...]
    s = cs * _sigmoid(cs)
    o_ref[0, 0] = jnp.dot(s, w_ref[0], preferred_element_type=f32,
                          precision=lax.Precision.HIGHEST) + b_ref[0]


def _modulation(cm, w_mod, b_mod):
    depth, d, _ = w_mod.shape
    rows = cm.shape[0]
    return pl.pallas_call(
        _mod_kernel,
        grid=(depth, 6),
        in_specs=[
            pl.BlockSpec((rows, d), lambda l, s: (0, 0)),
            pl.BlockSpec((1, d, d), lambda l, s: (l, 0, s)),
            pl.BlockSpec((1, 1, d), lambda l, s: (l, 0, s)),
        ],
        out_specs=pl.BlockSpec((1, 1, rows, d), lambda l, s: (l, s, 0, 0)),
        out_shape=jax.ShapeDtypeStruct((depth, 6, rows, d), f32),
        compiler_params=pltpu.CompilerParams(
            dimension_semantics=("arbitrary", "arbitrary"), vmem_limit_bytes=VMEM_LIMIT),
        name="modulation",
    )(cm, w_mod, b_mod.reshape(depth, 1, 6 * d))


def _embed_kernel(ctx_ref, x_ref, pos_ref, o_ref, *, nct):
    i = pl.program_id(0)

    @pl.when(i < nct)
    def _():
        o_ref[...] = _layer_norm(ctx_ref[...])

    @pl.when(i >= nct)
    def _():
        o_ref[...] = _layer_norm(x_ref[...] + pos_ref[...])


def _embed(ctx_flat, x_flat, pos, tm):
    d = x_flat.shape[1]
    nct = ctx_flat.shape[0] // tm
    nlt = x_flat.shape[0] // tm
    npos = pos.shape[0] // tm
    nbatch = nlt // npos

    def lat(i):
        s = jnp.maximum(i - nct, 0)
        return (s % nbatch) * npos + s // nbatch

    return pl.pallas_call(
        functools.partial(_embed_kernel, nct=nct),
        grid=(nct + nlt,),
        in_specs=[
            pl.BlockSpec((tm, d), lambda i: (jnp.minimum(i, nct - 1), 0)),
            pl.BlockSpec((tm, d), lambda i: (lat(i), 0)),
            pl.BlockSpec((tm, d), lambda i: (jnp.maximum(i - nct, 0) // nbatch, 0)),
        ],
        out_specs=pl.BlockSpec((tm, d), lambda i: (jnp.where(i < nct, i, nct + lat(i)), 0)),
        out_shape=jax.ShapeDtypeStruct(((nct + nlt) * tm, d), f32),
        compiler_params=pltpu.CompilerParams(
            dimension_semantics=("arbitrary",), vmem_limit_bytes=VMEM_LIMIT),
        name="embed",
    )(ctx_flat, x_flat, pos)


def _in_proj_kernel(x_ref, sh_ref, sc_ref, wa_ref, wb_ref, wz_ref, pm_ref, cnt_ref, wpool_ref, pscale_ref,
                    pb_ref, q_ref, k_ref, v_ref, rs_ref, sgp_ref, sgg_ref, z_ref, *, tm):
    u = (x_ref[...] * (1.0 + sc_ref[0]) + sh_ref[0]).astype(bf16)

    def proj(a, n):
        if a >= C_Z:
            return _dot(u, wz_ref[:, a - C_Z:a - C_Z + n])
        if a >= C_GP:
            return _dot(u, wb_ref[:, a - C_GP:a - C_GP + n])
        return _dot(u, wa_ref[:, a:a + n])

    def segment(dst_ref, col0, width, fn):
        for j in range(width // PROJ_CHUNK):
            cs = slice(j * PROJ_CHUNK, (j + 1) * PROJ_CHUNK)
            dst_ref[:, cs] = fn(proj(col0 + j * PROJ_CHUNK, PROJ_CHUNK)).astype(bf16)

    gpc = PROJ_CHUNK // POOL_GS
    p_parts = [proj(C_POOL + j * PROJ_CHUNK, PROJ_CHUNK) for j in range(POOL_WIDTH // PROJ_CHUNK)]
    segment(q_ref, C_Q, QK_W, lambda t: t * (GLA_DK ** -0.5))
    segment(k_ref, C_K, QK_W, lambda t: t)
    pooled = []
    for g in range(POOL_GROUPS):
        gc = slice((g % gpc) * POOL_GS, (g % gpc + 1) * POOL_GS)
        pg = jnp.concatenate([p_parts[g // gpc][:POOL_BLOCK, gc], p_parts[g // gpc][POOL_BLOCK:, gc]], axis=1)
        hi, lo = _split_bf16(pg)
        s = _dot(pm_ref[0, g], jnp.concatenate([hi, lo], axis=0))
        pooled.append((s, pg))
    segment(v_ref, C_V, V_W, lambda t: t)
    for g in range(POOL_GROUPS):
        cols = slice(g * POOL_GS, (g + 1) * POOL_GS)
        s, pg = pooled[g]
        y = s / cnt_ref[0, :, 2 * g * POOL_GS:2 * (g + 1) * POOL_GS] - pg
        y = jnp.concatenate([y[:, :POOL_GS], y[:, POOL_GS:]], axis=0).astype(bf16)
        pb_ref[:, cols] = (_dot(y, wpool_ref[g]) * pscale_ref[:, cols]).astype(bf16)
    segment(rs_ref, C_R, V_W, lambda t: t * _sigmoid(t))
    segment(sgp_ref, C_GP, D_MODEL, _sigmoid)
    segment(sgg_ref, C_GG, D_MODEL, _sigmoid)
    z3 = proj(C_Z, LANES)
    z_hi, z_lo = _split_bf16(z3)
    lane = lax.broadcasted_iota(jnp.int32, z3.shape, 1)
    in_lo = (lane >= Z_COPY) & (lane < 2 * Z_COPY)
    z_ref[...] = jnp.where(in_lo, z_lo, z_hi)


def _in_proj(xs, mod, w_in_p, pmats, cnts, w_pool_b, pool_scale, *, layer, tm, nct, tpb, nb, mod_rows):
    n, d = xs.shape
    nt = n // tm

    def mod_idx(slot):
        def idx(i):
            row = jnp.where(i < nct, nb, jnp.maximum(i - nct, 0) // tpb)
            return ((layer * 6 + slot) * mod_rows + row, 0, 0)
        return idx

    def kind(i):
        return jnp.where(i < nct, 1, 0)

    tok = lambda w: pl.BlockSpec((tm, w), lambda i: (i, 0))
    const1 = pl.Buffered(1)
    out_shapes = (
        jax.ShapeDtypeStruct((n, POOL_WIDTH), bf16),
        jax.ShapeDtypeStruct((n, QK_W), bf16),
        jax.ShapeDtypeStruct((n, QK_W), bf16),
        jax.ShapeDtypeStruct((n, V_W), bf16),
        jax.ShapeDtypeStruct((n, V_W), bf16),
        jax.ShapeDtypeStruct((n, D_MODEL), bf16),
        jax.ShapeDtypeStruct((n, D_MODEL), bf16),
        jax.ShapeDtypeStruct((n, LANES), bf16),
    )
    return pl.pallas_call(
        functools.partial(_in_proj_kernel, tm=tm),
        grid=(nt,),
        in_specs=[
            tok(d),
            pl.BlockSpec((1, 1, d), mod_idx(0)),
            pl.BlockSpec((1, 1, d), mod_idx(1)),
            pl.BlockSpec((None, d, C_GP), lambda i: (layer, 0, 0), pipeline_mode=const1),
            pl.BlockSpec((None, d, C_Z - C_GP), lambda i: (layer, 0, 0), pipeline_mode=const1),
            pl.BlockSpec((None, d, LANES), lambda i: (layer, 0, 0), pipeline_mode=const1),
            pl.BlockSpec((1, POOL_GROUPS, POOL_BLOCK, 2 * POOL_BLOCK), lambda i: (kind(i), 0, 0, 0)),
            pl.BlockSpec((1, POOL_BLOCK, 2 * POOL_WIDTH), lambda i: (kind(i), 0, 0)),
            pl.BlockSpec((None, POOL_GROUPS, POOL_GS, POOL_GS), lambda i: (layer, 0, 0, 0)),
            pl.BlockSpec((None, 1, POOL_WIDTH), lambda i: (layer, 0, 0)),
        ],
        out_specs=[tok(POOL_WIDTH), tok(QK_W), tok(QK_W), tok(V_W), tok(V_W),
                   tok(D_MODEL), tok(D_MODEL), tok(LANES)],
        out_shape=out_shapes,
        compiler_params=pltpu.CompilerParams(
            dimension_semantics=("arbitrary",), vmem_limit_bytes=VMEM_LIMIT),
        name="in_proj",
    )(xs, mod, mod, *w_in_p, pmats, cnts, w_pool_b, pool_scale)


def _ordering_zero(t):
    bits = lax.bitcast_convert_type(t, jnp.uint32)
    bits = lax.shift_right_logical(lax.shift_right_logical(bits, jnp.uint32(16)), jnp.uint32(16))
    return lax.bitcast_convert_type(bits, f32)


def _gla_mask(reverse):
    c = GLA_CHUNK
    row = lax.broadcasted_iota(jnp.int32, (c, c), 0)
    col = lax.broadcasted_iota(jnp.int32, (c, c), 1)
    return (col >= row) if reverse else (col <= row)


def _gla_log_decay(pre, bias):
    pre = pre + bias
    scale = LOG2E / GLA_GATE_TEMP
    la = jnp.minimum(pre, 0.0) * scale - jnp.log(1.0 + jnp.exp(-jnp.abs(pre))) * scale
    la_hi, la_lo = _split_bf16(la)
    return jnp.concatenate([la_hi, la_lo], axis=0), _ordering_zero(la_lo[0:1, :].astype(f32))


def _gla_operands(q_ref, k_ref, rows, bc, after, *, reverse):
    c = GLA_CHUNK
    if reverse:
        b_end = bc[0:1, :]
        m = bc[c // 2:c // 2 + 1, :]
    else:
        b_end = bc[c - 1:c, :]
        m = bc[c // 2 - 1:c // 2, :]
    m = m + after
    qe = q_ref[rows, :] * jnp.exp2(bc - m).astype(bf16)
    ke = k_ref[rows, :] * jnp.exp2(m - bc).astype(bf16)
    q_full = qe * jnp.exp2(m).astype(bf16)
    k_end = ke * jnp.exp2(b_end - m).astype(bf16)
    ends = [b_end[:, h * GLA_DK:(h + 1) * GLA_DK] for h in range(GLA_HEADS)]
    end_rows = jnp.concatenate(ends + [jnp.zeros((GLA_DK - GLA_HEADS, GLA_DK), f32)], axis=0)
    dec_cols = jnp.exp2(end_rows.T)
    done = _ordering_zero((q_full[0:1, :] + k_end[0:1, :]).astype(f32))
    return (_gla_mask(reverse), qe, ke, q_full, k_end, dec_cols), done


def _gla_apply(prep, v_ref, o_ref, rows, states):
    c = GLA_CHUNK
    keep, qe, ke, q_full, k_end, dec_cols = prep
    heads = range(GLA_HEADS)
    kcs = [slice(h * GLA_DK, (h + 1) * GLA_DK) for h in heads]
    vcs = [slice(h * GLA_DV, (h + 1) * GLA_DV) for h in heads]
    att = [lax.dot_general(qe[:, kcs[h]], ke[:, kcs[h]], (((1,), (1,)), ((), ())), preferred_element_type=f32)
           for h in heads]
    kv = [_dot(k_end[:, kcs[h]].T, v_ref[rows, vcs[h]]) for h in heads]
    new_states = []
    for h in heads:
        a = jnp.where(keep, att[h], 0.0).astype(bf16)
        lhs = jnp.concatenate([a, q_full[:, kcs[h]]], axis=1)
        rhs = jnp.concatenate([v_ref[rows, vcs[h]], states[h].astype(bf16)], axis=0)
        o_ref[rows, vcs[h]] = _dot(lhs, rhs).astype(o_ref.dtype)
        new_states.append(states[h] * dec_cols[:, h:h + 1] + kv[h])
    return new_states


def _gla_kernel(qf_ref, kf_ref, vf_ref, zf_ref, qb_ref, kb_ref, vb_ref, zb_ref, wg_ref, bg_ref,
                of_ref, ob_ref, sf_ref, sb_ref, *, nchunks):
    @pl.when(pl.program_id(1) == 0)
    def _():
        sf_ref[...] = jnp.zeros_like(sf_ref)
        sb_ref[...] = jnp.zeros_like(sb_ref)

    c = GLA_CHUNK
    wg_f, wg_b = wg_ref[:, :QK_W], wg_ref[:, QK_W:]
    bias_f, bias_b = bg_ref[:, :QK_W], bg_ref[:, QK_W:]
    s_f = [sf_ref[h] for h in range(GLA_HEADS)]
    s_b = [sb_ref[h] for h in range(GLA_HEADS)]
    units = []
    for i in range(nchunks):
        units.append((False, slice(i * c, (i + 1) * c)))
        units.append((True, slice((nchunks - 1 - i) * c, (nchunks - i) * c)))
    refs = {False: (qf_ref, kf_ref, vf_ref, zf_ref, of_ref, wg_f, bias_f),
            True: (qb_ref, kb_ref, vb_ref, zb_ref, ob_ref, wg_b, bias_b)}
    tri2 = {r: jnp.concatenate([jnp.where(_gla_mask(r), 1.0, 0.0).astype(bf16)] * 2, axis=1) for r in (False, True)}

    pre = [_dot(refs[r][3][rows, :], refs[r][5]) for r, rows in units]
    n = len(units)
    bcs, preps = [None] * n, [None] * n
    la_done = jnp.zeros((1, QK_W), f32)
    op_done = jnp.zeros((1, QK_W), f32)

    def running_sum(u):
        nonlocal la_done
        la, la_done = _gla_log_decay(pre[u], refs[units[u][0]][6] + la_done)
        bcs[u] = _dot(tri2[units[u][0]], la)

    def operands(u):
        nonlocal op_done
        r, rows = units[u]
        preps[u], op_done = _gla_operands(refs[r][0], refs[r][1], rows, bcs[u], op_done, reverse=r)

    states = {False: s_f, True: s_b}
    running_sum(0)
    if n > 1:
        running_sum(1)
    operands(0)
    for u in range(n):
        r, rows = units[u]
        if u + 1 < n:
            operands(u + 1)
        states[r] = _gla_apply(preps[u], refs[r][2], refs[r][4], rows, states[r])
        if u + 2 < n:
            running_sum(u + 2)
    for h in range(GLA_HEADS):
        sf_ref[h] = states[False][h]
        sb_ref[h] = states[True][h]


def _gla(q, k, v, z, wg, bg, *, layer, nb, nbc, nbl):
    n = q.shape[0]
    gb = GLA_BLOCK

    def fwd_blk(b, s):
        return jnp.where(s < nbc, b * nbc + s, nb * nbc + b * nbl + (s - nbc))

    def bwd_blk(b, s):
        return jnp.where(s < nbc, b * nbc + (nbc - 1 - s), nb * nbc + b * nbl + (nbl - 1 - (s - nbc)))

    def spec(w, blk):
        return pl.BlockSpec((gb, w), lambda b, s: (blk(b, s), 0))

    ins = []
    for blk in (fwd_blk, bwd_blk):
        ins += [spec(QK_W, blk), spec(QK_W, blk), spec(V_W, blk), spec(LANES, blk)]
    ins += [pl.BlockSpec((None, LANES, 2 * QK_W), lambda b, s: (layer, 0, 0)),
            pl.BlockSpec((None, 1, 2 * QK_W), lambda b, s: (layer, 0, 0))]
    return pl.pallas_call(
        functools.partial(_gla_kernel, nchunks=gb // GLA_CHUNK),
        grid=(nb, nbc + nbl),
        in_specs=ins,
        out_specs=[spec(V_W, fwd_blk), spec(V_W, bwd_blk)],
        out_shape=(jax.ShapeDtypeStruct((n, V_W), bf16), jax.ShapeDtypeStruct((n, V_W), bf16)),
        scratch_shapes=[pltpu.VMEM((GLA_HEADS, GLA_DK, GLA_DV), f32),
                        pltpu.VMEM((GLA_HEADS, GLA_DK, GLA_DV), f32)],
        compiler_params=pltpu.CompilerParams(
            dimension_semantics=("arbitrary", "arbitrary"), vmem_limit_bytes=VMEM_LIMIT),
        name="gla",
    )(q, k, v, z, q, k, v, z, wg, bg)


def _mixer_kernel(x_ref, of_ref, ob_ref, rs_ref, pb_ref, sgp_ref, sgg_ref, g1_ref, gnw_ref,
                  wbg_ref, wbp_ref, wout_ref, lnw_ref, lnb_ref, out_ref, *, alpha):
    o = of_ref[...].astype(f32) + ob_ref[...].astype(f32)
    heads = []
    for h in range(GLA_HEADS):
        oh = o[:, h * GLA_DV:(h + 1) * GLA_DV]
        ms = jnp.mean(oh * oh, axis=-1, keepdims=True)
        heads.append(oh * lax.rsqrt(ms + LN_EPS))
    of = jnp.concatenate(heads, axis=1) * gnw_ref[...]
    y_gla = _dot((of * rs_ref[...].astype(f32)).astype(bf16), wbg_ref[...])
    y_pool = _dot(pb_ref[...], wbp_ref[...])
    merged = sgp_ref[...].astype(f32) * y_pool + sgg_ref[...].astype(f32) * y_gla
    mix = _dot(merged.astype(bf16), wout_ref[...])
    t = alpha * x_ref[...] + g1_ref[0] * mix
    out_ref[...] = _layer_norm(t) * lnw_ref[...] + lnb_ref[...]


def _mixer(xs, o_f, o_b, rs, pb, sgp, sgg, mod, gnw, wbg, wbp, wout, lnw, lnb, *,
           layer, tm, nct, tpb, nb, mod_rows, off, alpha):
    n, d = xs.shape
    nt = n // tm - off

    def mod_idx(i):
        j = i + off
        row = jnp.where(j < nct, nb, jnp.maximum(j - nct, 0) // tpb)
        return ((layer * 6 + 2) * mod_rows + row, 0, 0)

    tok = lambda w: pl.BlockSpec((tm, w), lambda i: (i + off, 0))
    const1 = pl.Buffered(1)
    vec = lambda w: pl.BlockSpec((None, 1, w), lambda i: (layer, 0, 0))
    mat = lambda a, b: pl.BlockSpec((None, a, b), lambda i: (layer, 0, 0), pipeline_mode=const1)
    return pl.pallas_call(
        functools.partial(_mixer_kernel, alpha=alpha),
        grid=(nt,),
        in_specs=[tok(d), tok(V_W), tok(V_W), tok(V_W), tok(POOL_WIDTH), tok(d), tok(d),
                  pl.BlockSpec((1, 1, d), mod_idx), vec(V_W),
                  mat(V_W, d), mat(POOL_WIDTH, d), mat(d, d), vec(d), vec(d)],
        out_specs=pl.BlockSpec((tm, d), lambda i: (i, 0)),
        out_shape=jax.ShapeDtypeStruct((nt * tm, d), f32),
        compiler_params=pltpu.CompilerParams(
            dimension_semantics=("arbitrary",), vmem_limit_bytes=VMEM_LIMIT),
        name="mixer",
    )(xs, o_f, o_b, rs, pb, sgp, sgg, mod, gnw, wbg, wbp, wout, lnw, lnb)


def _ffn_kernel(x_ref, sh_ref, sc_ref, g2_ref, wup_ref, cw_ref, cb_ref, wdn_ref, lnw_ref, lnb_ref,
                out_ref, hid_ref, *, alpha, tm, nct, off, ctx_len):
    x = x_ref[...]
    u = (x * (1.0 + sc_ref[0]) + sh_ref[0]).astype(bf16)

    is_ctx = (pl.program_id(0) + off) < nct
    row_len = jnp.where(is_ctx, ctx_len, GRID_W)
    pos = lax.broadcasted_iota(jnp.int32, (tm, 1), 0) & (row_len - 1)
    has_prev = jnp.where(pos != 0, 1.0, 0.0)
    has_next = jnp.where(pos != row_len - 1, 1.0, 0.0)

    for c in range(FFN_HIDDEN // FFN_CHUNK):
        cs = slice(c * FFN_CHUNK, (c + 1) * FFN_CHUNK)
        a = _dot(u, wup_ref[:, cs])
        gate = _dot(u, wup_ref[:, FFN_HIDDEN + c * FFN_CHUNK:FFN_HIDDEN + (c + 1) * FFN_CHUNK])
        a_prev = pltpu.roll(a, 1, 0) * has_prev
        a_next = pltpu.roll(a, tm - 1, 0) * has_next
        a = cw_ref[0:1, cs] * a_prev + cw_ref[1:2, cs] * a + cw_ref[2:3, cs] * a_next + cb_ref[:, cs]
        hid = 0.5 * a * (1.0 + lax.erf(a * (2.0 ** -0.5))) * gate
        hid_ref[:, cs] = hid.astype(bf16)
    t = alpha * x + g2_ref[0] * _dot(hid_ref[...], wdn_ref[...])
    out_ref[...] = _layer_norm(t) * lnw_ref[...] + lnb_ref[...]


def _ffn(xs, mod, wup, cw, cb, wdn, lnw, lnb, *, layer, tm, nct, tpb, nb, mod_rows, off, alpha, ctx_len):
    n, d = xs.shape
    nt = n // tm

    def mod_idx(slot):
        def idx(i):
            j = i + off
            row = jnp.where(j < nct, nb, jnp.maximum(j - nct, 0) // tpb)
            return ((layer * 6 + slot) * mod_rows + row, 0, 0)
        return idx

    const1 = pl.Buffered(1)
    vec = lambda w: pl.BlockSpec((None, 1, w), lambda i: (layer, 0, 0))
    return pl.pallas_call(
        functools.partial(_ffn_kernel, alpha=alpha, tm=tm, nct=nct, off=off, ctx_len=ctx_len),
        grid=(nt,),
        in_specs=[pl.BlockSpec((tm, d), lambda i: (i, 0)),
                  pl.BlockSpec((1, 1, d), mod_idx(3)),
                  pl.BlockSpec((1, 1, d), mod_idx(4)),
                  pl.BlockSpec((1, 1, d), mod_idx(5)),
                  pl.BlockSpec((None, d, 2 * FFN_HIDDEN), lambda i: (layer, 0, 0), pipeline_mode=const1),
                  pl.BlockSpec((None, 3, FFN_HIDDEN), lambda i: (layer, 0, 0)),
                  vec(FFN_HIDDEN),
                  pl.BlockSpec((None, FFN_HIDDEN, d), lambda i: (layer, 0, 0), pipeline_mode=const1),
                  vec(d), vec(d)],
        out_specs=pl.BlockSpec((tm, d), lambda i: (i, 0)),
        out_shape=jax.ShapeDtypeStruct((n, d), f32),
        scratch_shapes=[pltpu.VMEM((tm, FFN_HIDDEN), bf16)],
        compiler_params=pltpu.CompilerParams(
            dimension_semantics=("arbitrary",), vmem_limit_bytes=VMEM_LIMIT),
        name="ffn",
    )(xs, mod, mod, mod, wup, cw, cb, wdn, lnw, lnb)


def _pos_embed_2d(rows, cols, dim):
    quarter = dim // 4
    omega = 1.0 / (10000.0 ** (jnp.arange(quarter, dtype=f32) / quarter))
    r = jnp.arange(rows, dtype=f32)[:, None] * omega
    cl = jnp.arange(cols, dtype=f32)[:, None] * omega
    er = jnp.concatenate([jnp.sin(r), jnp.cos(r)], -1)
    ec = jnp.concatenate([jnp.sin(cl), jnp.cos(cl)], -1)
    emb = jnp.concatenate([jnp.broadcast_to(er[:, None, :], (rows, cols, dim // 2)),
                           jnp.broadcast_to(ec[None, :, :], (rows, cols, dim // 2))], -1)
    return emb.reshape(rows * cols, dim)


def _pool_tables(ctx_len):
    t = np.arange(POOL_BLOCK)
    mats = np.zeros((2, POOL_GROUPS, POOL_BLOCK, 2 * POOL_BLOCK), np.float32)
    cnts = np.zeros((2, POOL_BLOCK, 2 * POOL_WIDTH), np.float32)
    for kind, row_len in enumerate((GRID_W, ctx_len)):
        base = (t // row_len) * row_len
        r = t - base
        for g, w in enumerate(POOL_WINDOWS):
            lo = np.clip(r - w // 2, 0, row_len) + base
            hi = np.clip(r + w // 2, 0, row_len) + base
            band = (t[None, :] >= lo[:, None]) & (t[None, :] < hi[:, None])
            mats[kind, g] = np.concatenate([band, band], axis=1)
            cnts[kind, :, 2 * g * POOL_GS:2 * (g + 1) * POOL_GS] = (hi - lo)[:, None]
    return jnp.asarray(mats, bf16), jnp.asarray(cnts, f32)


def kernel(x, c, ctx, c_ctx, w_mod, b_mod, w_in, w_gate_f, b_gate_f, w_gate_b, b_gate_b, gla_norm_w, w_pool, pool_scale, w_br_pool, w_br_gla, w_out, ln1_w, ln1_b, w_up, conv_w, conv_b, w_down, ln2_w, ln2_b):
    nb, seq, d = x.shape
    ctx_len = ctx.shape[1]
    depth = w_mod.shape[0]
    tm = TOKEN_TILE
    assert d == D_MODEL and seq % GRID_W == 0
    assert ctx_len == POOL_BLOCK and ctx_len % GLA_BLOCK == 0 and seq % GLA_BLOCK == 0
    assert (nb * ctx_len) % tm == 0 and seq % tm == 0 and tm == 2 * POOL_BLOCK
    alpha = (2.0 * depth) ** 0.25
    nct = nb * ctx_len // tm
    tpb = seq // tm
    nbc, nbl = ctx_len // GLA_BLOCK, seq // GLA_BLOCK

    mod_rows = -(-(nb + 1) // 8) * 8
    cm = jnp.zeros((mod_rows, d), f32).at[:nb].set(c).at[nb].set(c_ctx)
    mod = _modulation(cm, w_mod, b_mod).reshape(depth * 6 * mod_rows, 1, d)

    zcols = w_in[:, :, C_GP:C_GP + Z_COPY].astype(bf16)
    w_in_p = (w_in[:, :, :C_GP].astype(bf16), w_in[:, :, C_GP + Z_COPY:].astype(bf16),
              jnp.concatenate([zcols, zcols, zcols, jnp.zeros((depth, d, LANES - 3 * Z_COPY), bf16)], axis=-1))
    wg32 = jnp.zeros((depth, Z_COPY, 2 * QK_W), f32)
    wg32 = wg32.at[:, :GLA_GATE_RANK, :QK_W].set(w_gate_f)
    wg32 = wg32.at[:, GLA_GATE_RANK:, QK_W:].set(w_gate_b)
    wg_hi = wg32.astype(bf16)
    wg_lo = (wg32 - wg_hi.astype(f32)).astype(bf16)
    wg = jnp.concatenate([wg_hi, wg_hi, wg_lo,
                          jnp.zeros((depth, LANES - 3 * Z_COPY, 2 * QK_W), bf16)], axis=1)
    bg = jnp.concatenate([b_gate_f, b_gate_b], axis=-1).reshape(depth, 1, 2 * QK_W)
    w_pool_b = w_pool.astype(bf16)
    wbp, wbg, wo = w_br_pool.astype(bf16), w_br_gla.astype(bf16), w_out.astype(bf16)
    wup, wdn = w_up.astype(bf16), w_down.astype(bf16)
    row3 = lambda a: a.reshape(depth, 1, a.shape[-1])
    pmats, cnts = _pool_tables(ctx_len)

    pos = _pos_embed_2d(seq // GRID_W, GRID_W, d)
    xs = _embed(ctx.reshape(nb * ctx_len, d), x.reshape(nb * seq, d), pos, tm)

    common = dict(tm=tm, nct=nct, tpb=tpb, nb=nb, mod_rows=mod_rows)
    for l in range(depth):
        off = nct if l == depth - 1 else 0
        pb, q, k, v, rs, sgp, sgg, z = _in_proj(xs, mod, w_in_p, pmats, cnts, w_pool_b, row3(pool_scale),
                                                  layer=l, **common)
        o_f, o_b = _gla(q, k, v, z, wg, bg, layer=l, nb=nb, nbc=nbc, nbl=nbl)
        xs = _mixer(xs, o_f, o_b, rs, pb, sgp, sgg, mod, row3(gla_norm_w), wbg, wbp, wo,
                    row3(ln1_w), row3(ln1_b), layer=l, off=off, alpha=alpha, **common)
        xs = _ffn(xs, mod, wup, conv_w, row3(conv_b), wdn, row3(ln2_w), row3(ln2_b),
                  layer=l, off=off, alpha=alpha, ctx_len=ctx_len, **common)
    return xs.reshape(nb, seq, d)
```

```python
import functools

import numpy as np
import jax
import jax.numpy as jnp
from jax import lax
from jax.experimental import pallas as pl
from jax.experimental.pallas import tpu as pltpu

f32 = jnp.float32
bf16 = jnp.bfloat16

D_MODEL = 1024
GRID_W = 64
POOL_WIDTH = 512
POOL_GROUPS = 4
POOL_GS = POOL_WIDTH // POOL_GROUPS
POOL_WINDOWS = (2, 4, 8, 16)
GLA_HEADS = 4
GLA_DK = 128
GLA_DV = 256
QK_W = GLA_HEADS * GLA_DK
V_W = GLA_HEADS * GLA_DV
GLA_GATE_RANK = 16
GLA_GATE_TEMP = 16.0
FFN_HIDDEN = 2816
LN_EPS = 1e-6

LANES = 128
POOL_BLOCK = 256
GLA_CHUNK = 128
GLA_BLOCK = 256
FFN_CHUNK = 256
PROJ_CHUNK = 256
TOKEN_TILE = 512
VMEM_LIMIT = 56 * 1024 * 1024

C_POOL, C_Q, C_K, C_V, C_R, C_GP, C_GG, C_Z = 0, 512, 1024, 1536, 2560, 3584, 4608, 5632
Z_COPY = 2 * GLA_GATE_RANK
LOG2E = 1.4426950408889634


def _sigmoid(t):
    return 1.0 / (1.0 + jnp.exp(-t))


def _layer_norm(t):
    mu = jnp.mean(t, axis=-1, keepdims=True)
    d = t - mu
    var = jnp.mean(d * d, axis=-1, keepdims=True)
    return d * lax.rsqrt(var + LN_EPS)


def _split_bf16(t):
    hi = t.astype(bf16)
    lo = (t - hi.astype(f32)).astype(bf16)
    return hi, lo


def _dot(a, b):
    return jnp.dot(a, b, preferred_element_type=f32)


def _mod_kernel(c_ref, w_ref, b_ref, o_ref):
    cs = c_ref[...]
    s = cs * _sigmoid(cs)
    o_ref[0, 0] = jnp.dot(s, w_ref[0], preferred_element_type=f32,
                          precision=lax.Precision.HIGHEST) + b_ref[0]


def _modulation(cm, w_mod, b_mod):
    depth, d, _ = w_mod.shape
    rows = cm.shape[0]
    return pl.pallas_call(
        _mod_kernel,
        grid=(depth, 6),
        in_specs=[
            pl.BlockSpec((rows, d), lambda l, s: (0, 0)),
            pl.BlockSpec((1, d, d), lambda l, s: (l, 0, s)),
            pl.BlockSpec((1, 1, d), lambda l, s: (l, 0, s)),
        ],
        out_specs=pl.BlockSpec((1, 1, rows, d), lambda l, s: (l, s, 0, 0)),
        out_shape=jax.ShapeDtypeStruct((depth, 6, rows, d), f32),
        compiler_params=pltpu.CompilerParams(
            dimension_semantics=("arbitrary", "arbitrary"), vmem_limit_bytes=VMEM_LIMIT),
        name="modulation",
    )(cm, w_mod, b_mod.reshape(depth, 1, 6 * d))


def _embed_kernel(ctx_ref, x_ref, pos_ref, o_ref, *, nct):
    i = pl.program_id(0)

    @pl.when(i < nct)
    def _():
        o_ref[...] = _layer_norm(ctx_ref[...])

    @pl.when(i >= nct)
    def _():
        o_ref[...] = _layer_norm(x_ref[...] + pos_ref[...])


def _embed(ctx_flat, x_flat, pos, tm):
    d = x_flat.shape[1]
    nct = ctx_flat.shape[0] // tm
    nlt = x_flat.shape[0] // tm
    npos = pos.shape[0] // tm
    nbatch = nlt // npos

    def lat(i):
        s = jnp.maximum(i - nct, 0)
        return (s % nbatch) * npos + s // nbatch

    return pl.pallas_call(
        functools.partial(_embed_kernel, nct=nct),
        grid=(nct + nlt,),
        in_specs=[
            pl.BlockSpec((tm, d), lambda i: (jnp.minimum(i, nct - 1), 0)),
            pl.BlockSpec((tm, d), lambda i: (lat(i), 0)),
            pl.BlockSpec((tm, d), lambda i: (jnp.maximum(i - nct, 0) // nbatch, 0)),
        ],
        out_specs=pl.BlockSpec((tm, d), lambda i: (jnp.where(i < nct, i, nct + lat(i)), 0)),
        out_shape=jax.ShapeDtypeStruct(((nct + nlt) * tm, d), f32),
        compiler_params=pltpu.CompilerParams(
            dimension_semantics=("arbitrary",), vmem_limit_bytes=VMEM_LIMIT),
        name="embed",
    )(ctx_flat, x_flat, pos)


def _in_proj_kernel(x_ref, sh_ref, sc_ref, wa_ref, wb_ref, wz_ref, pm_ref, cnt_ref, wpool_ref, pscale_ref,
                    pb_ref, q_ref, k_ref, v_ref, rs_ref, sgp_ref, sgg_ref, z_ref, *, tm):
    u = (x_ref[...] * (1.0 + sc_ref[0]) + sh_ref[0]).astype(bf16)

    def proj(a, n):
        if a >= C_Z:
            return _dot(u, wz_ref[:, a - C_Z:a - C_Z + n])
        if a >= C_GP:
            return _dot(u, wb_ref[:, a - C_GP:a - C_GP + n])
        return _dot(u, wa_ref[:, a:a + n])

    def segment(dst_ref, col0, width, fn):
        for j in range(width // PROJ_CHUNK):
            cs = slice(j * PROJ_CHUNK, (j + 1) * PROJ_CHUNK)
            dst_ref[:, cs] = fn(proj(col0 + j * PROJ_CHUNK, PROJ_CHUNK)).astype(bf16)

    gpc = PROJ_CHUNK // POOL_GS
    p_parts = [proj(C_POOL + j * PROJ_CHUNK, PROJ_CHUNK) for j in range(POOL_WIDTH // PROJ_CHUNK)]
    segment(q_ref, C_Q, QK_W, lambda t: t * (GLA_DK ** -0.5))
    segment(k_ref, C_K, QK_W, lambda t: t)
    pooled = []
    for g in range(POOL_GROUPS):
        gc = slice((g % gpc) * POOL_GS, (g % gpc + 1) * POOL_GS)
        pg = jnp.concatenate([p_parts[g // gpc][:POOL_BLOCK, gc], p_parts[g // gpc][POOL_BLOCK:, gc]], axis=1)
        hi, lo = _split_bf16(pg)
        s = _dot(pm_ref[0, g], jnp.concatenate([hi, lo], axis=0))
        pooled.append((s, pg))
    segment(v_ref, C_V, V_W, lambda t: t)
    for g in range(POOL_GROUPS):
        cols = slice(g * POOL_GS, (g + 1) * POOL_GS)
        s, pg = pooled[g]
        y = s / cnt_ref[0, :, 2 * g * POOL_GS:2 * (g + 1) * POOL_GS] - pg
        y = jnp.concatenate([y[:, :POOL_GS], y[:, POOL_GS:]], axis=0).astype(bf16)
        pb_ref[:, cols] = (_dot(y, wpool_ref[g]) * pscale_ref[:, cols]).astype(bf16)
    segment(rs_ref, C_R, V_W, lambda t: t * _sigmoid(t))
    segment(sgp_ref, C_GP, D_MODEL, _sigmoid)
    segment(sgg_ref, C_GG, D_MODEL, _sigmoid)
    z3 = proj(C_Z, LANES)
    z_hi, z_lo = _split_bf16(z3)
    lane = lax.broadcasted_iota(jnp.int32, z3.shape, 1)
    in_lo = (lane >= Z_COPY) & (lane < 2 * Z_COPY)
    z_ref[...] = jnp.where(in_lo, z_lo, z_hi)


def _in_proj(xs, mod, w_in_p, pmats, cnts, w_pool_b, pool_scale, *, layer, tm, nct, tpb, nb, mod_rows):
    n, d = xs.shape
    nt = n // tm

    def mod_idx(slot):
        def idx(i):
            row = jnp.where(i < nct, nb, jnp.maximum(i - nct, 0) // tpb)
            return ((layer * 6 + slot) * mod_rows + row, 0, 0)
        return idx

    def kind(i):
        return jnp.where(i < nct, 1, 0)

    tok = lambda w: pl.BlockSpec((tm, w), lambda i: (i, 0))
    const1 = pl.Buffered(1)
    out_shapes = (
        jax.ShapeDtypeStruct((n, POOL_WIDTH), bf16),
        jax.ShapeDtypeStruct((n, QK_W), bf16),
        jax.ShapeDtypeStruct((n, QK_W), bf16),
        jax.ShapeDtypeStruct((n, V_W), bf16),
        jax.ShapeDtypeStruct((n, V_W), bf16),
        jax.ShapeDtypeStruct((n, D_MODEL), bf16),
        jax.ShapeDtypeStruct((n, D_MODEL), bf16),
        jax.ShapeDtypeStruct((n, LANES), bf16),
    )
    return pl.pallas_call(
        functools.partial(_in_proj_kernel, tm=tm),
        grid=(nt,),
        in_specs=[
            tok(d),
            pl.BlockSpec((1, 1, d), mod_idx(0)),
            pl.BlockSpec((1, 1, d), mod_idx(1)),
            pl.BlockSpec((None, d, C_GP), lambda i: (layer, 0, 0), pipeline_mode=const1),
            pl.BlockSpec((None, d, C_Z - C_GP), lambda i: (layer, 0, 0), pipeline_mode=const1),
            pl.BlockSpec((None, d, LANES), lambda i: (layer, 0, 0), pipeline_mode=const1),
            pl.BlockSpec((1, POOL_GROUPS, POOL_BLOCK, 2 * POOL_BLOCK), lambda i: (kind(i), 0, 0, 0)),
            pl.BlockSpec((1, POOL_BLOCK, 2 * POOL_WIDTH), lambda i: (kind(i), 0, 0)),
            pl.BlockSpec((None, POOL_GROUPS, POOL_GS, POOL_GS), lambda i: (layer, 0, 0, 0)),
            pl.BlockSpec((None, 1, POOL_WIDTH), lambda i: (layer, 0, 0)),
        ],
        out_specs=[tok(POOL_WIDTH), tok(QK_W), tok(QK_W), tok(V_W), tok(V_W),
                   tok(D_MODEL), tok(D_MODEL), tok(LANES)],
        out_shape=out_shapes,
        compiler_params=pltpu.CompilerParams(
            dimension_semantics=("arbitrary",), vmem_limit_bytes=VMEM_LIMIT),
        name="in_proj",
    )(xs, mod, mod, *w_in_p, pmats, cnts, w_pool_b, pool_scale)


def _ordering_zero(t):
    bits = lax.bitcast_convert_type(t, jnp.uint32)
    bits = lax.shift_right_logical(lax.shift_right_logical(bits, jnp.uint32(16)), jnp.uint32(16))
    return lax.bitcast_convert_type(bits, f32)


def _gla_mask(reverse):
    c = GLA_CHUNK
    row = lax.broadcasted_iota(jnp.int32, (c, c), 0)
    col = lax.broadcasted_iota(jnp.int32, (c, c), 1)
    return (col >= row) if reverse else (col <= row)


def _gla_log_decay(pre, bias):
    pre = pre + bias
    scale = LOG2E / GLA_GATE_TEMP
    la = jnp.minimum(pre, 0.0) * scale - jnp.log(1.0 + jnp.exp(-jnp.abs(pre))) * scale
    la_hi, la_lo = _split_bf16(la)
    return jnp.concatenate([la_hi, la_lo], axis=0), _ordering_zero(la_lo[0:1, :].astype(f32))


def _gla_operands(q_ref, k_ref, rows, bc, after, *, reverse):
    c = GLA_CHUNK
    if reverse:
        b_end = bc[0:1, :]
        m = bc[c // 2:c // 2 + 1, :]
    else:
        b_end = bc[c - 1:c, :]
        m = bc[c // 2 - 1:c // 2, :]
    m = m + after
    qe = q_ref[rows, :] * jnp.exp2(bc - m).astype(bf16)
    ke = k_ref[rows, :] * jnp.exp2(m - bc).astype(bf16)
    q_full = qe * jnp.exp2(m).astype(bf16)
    k_end = ke * jnp.exp2(b_end - m).astype(bf16)
    ends = [b_end[:, h * GLA_DK:(h + 1) * GLA_DK] for h in range(GLA_HEADS)]
    end_rows = jnp.concatenate(ends + [jnp.zeros((GLA_DK - GLA_HEADS, GLA_DK), f32)], axis=0)
    dec_cols = jnp.exp2(end_rows.T)
    done = _ordering_zero((q_full[0:1, :] + k_end[0:1, :]).astype(f32))
    return (_gla_mask(reverse), qe, ke, q_full, k_end, dec_cols), done


def _gla_apply(prep, v_ref, o_ref, rows, states):
    c = GLA_CHUNK
    keep, qe, ke, q_full, k_end, dec_cols = prep
    heads = range(GLA_HEADS)
    kcs = [slice(h * GLA_DK, (h + 1) * GLA_DK) for h in heads]
    vcs = [slice(h * GLA_DV, (h + 1) * GLA_DV) for h in heads]
    att = [lax.dot_general(qe[:, kcs[h]], ke[:, kcs[h]], (((1,), (1,)), ((), ())), preferred_element_type=f32)
           for h in heads]
    kv = [_dot(k_end[:, kcs[h]].T, v_ref[rows, vcs[h]]) for h in heads]
    new_states = []
    for h in heads:
        a = jnp.where(keep, att[h], 0.0).astype(bf16)
        lhs = jnp.concatenate([a, q_full[:, kcs[h]]], axis=1)
        rhs = jnp.concatenate([v_ref[rows, vcs[h]], states[h].astype(bf16)], axis=0)
        o_ref[rows, vcs[h]] = _dot(lhs, rhs).astype(o_ref.dtype)
        new_states.append(states[h] * dec_cols[:, h:h + 1] + kv[h])
    return new_states


def _gla_kernel(qf_ref, kf_ref, vf_ref, zf_ref, qb_ref, kb_ref, vb_ref, zb_ref, wg_ref, bg_ref,
                of_ref, ob_ref, sf_ref, sb_ref, *, nchunks):
    @pl.when(pl.program_id(1) == 0)
    def _():
        sf_ref[...] = jnp.zeros_like(sf_ref)
        sb_ref[...] = jnp.zeros_like(sb_ref)

    c = GLA_CHUNK
    wg_f, wg_b = wg_ref[:, :QK_W], wg_ref[:, QK_W:]
    bias_f, bias_b = bg_ref[:, :QK_W], bg_ref[:, QK_W:]
    s_f = [sf_ref[h] for h in range(GLA_HEADS)]
    s_b = [sb_ref[h] for h in range(GLA_HEADS)]
    units = []
    for i in range(nchunks):
        units.append((False, slice(i * c, (i + 1) * c)))
        units.append((True, slice((nchunks - 1 - i) * c, (nchunks - i) * c)))
    refs = {False: (qf_ref, kf_ref, vf_ref, zf_ref, of_ref, wg_f, bias_f),
            True: (qb_ref, kb_ref, vb_ref, zb_ref, ob_ref, wg_b, bias_b)}
    tri2 = {r: jnp.concatenate([jnp.where(_gla_mask(r), 1.0, 0.0).astype(bf16)] * 2, axis=1) for r in (False, True)}

    pre = [_dot(refs[r][3][rows, :], refs[r][5]) for r, rows in units]
    n = len(units)
    bcs, preps = [None] * n, [None] * n
    la_done = jnp.zeros((1, QK_W), f32)
    op_done = jnp.zeros((1, QK_W), f32)

    def running_sum(u):
        nonlocal la_done
        la, la_done = _gla_log_decay(pre[u], refs[units[u][0]][6] + la_done)
        bcs[u] = _dot(tri2[units[u][0]], la)

    def operands(u):
        nonlocal op_done
        r, rows = units[u]
        preps[u], op_done = _gla_operands(refs[r][0], refs[r][1], rows, bcs[u], op_done, reverse=r)

    states = {False: s_f, True: s_b}
    running_sum(0)
    if n > 1:
        running_sum(1)
    operands(0)
    for u in range(n):
        r, rows = units[u]
        if u + 1 < n:
            operands(u + 1)
        states[r] = _gla_apply(preps[u], refs[r][2], refs[r][4], rows, states[r])
        if u + 2 < n:
            running_sum(u + 2)
    for h in range(GLA_HEADS):
        sf_ref[h] = states[False][h]
        sb_ref[h] = states[True][h]


def _gla(q, k, v, z, wg, bg, *, layer, nb, nbc, nbl):
    n = q.shape[0]
    gb = GLA_BLOCK

    def fwd_blk(b, s):
        return jnp.where(s < nbc, b * nbc + s, nb * nbc + b * nbl + (s - nbc))

    def bwd_blk(b, s):
        return jnp.where(s < nbc, b * nbc + (nbc - 1 - s), nb * nbc + b * nbl + (nbl - 1 - (s - nbc)))

    def spec(w, blk):
        return pl.BlockSpec((gb, w), lambda b, s: (blk(b, s), 0))

    ins = []
    for blk in (fwd_blk, bwd_blk):
        ins += [spec(QK_W, blk), spec(QK_W, blk), spec(V_W, blk), spec(LANES, blk)]
    ins += [pl.BlockSpec((None, LANES, 2 * QK_W), lambda b, s: (layer, 0, 0)),
            pl.BlockSpec((None, 1, 2 * QK_W), lambda b, s: (layer, 0, 0))]
    return pl.pallas_call(
        functools.partial(_gla_kernel, nchunks=gb // GLA_CHUNK),
        grid=(nb, nbc + nbl),
        in_specs=ins,
        out_specs=[spec(V_W, fwd_blk), spec(V_W, bwd_blk)],
        out_shape=(jax.ShapeDtypeStruct((n, V_W), bf16), jax.ShapeDtypeStruct((n, V_W), bf16)),
        scratch_shapes=[pltpu.VMEM((GLA_HEADS, GLA_DK, GLA_DV), f32),
                        pltpu.VMEM((GLA_HEADS, GLA_DK, GLA_DV), f32)],
        compiler_params=pltpu.CompilerParams(
            dimension_semantics=("arbitrary", "arbitrary"), vmem_limit_bytes=VMEM_LIMIT),
        name="gla",
    )(q, k, v, z, q, k, v, z, wg, bg)


def _post_kernel(x_ref, of_ref, ob_ref, rs_ref, pb_ref, sgp_ref, sgg_ref, g1_ref, sh2_ref, sc2_ref, g2_ref,
                 gnw_ref, wbg_ref, wbp_ref, wout_ref, ln1w_ref, ln1b_ref,
                 wup_ref, cw_ref, cb_ref, wdn_ref, ln2w_ref, ln2b_ref,
                 out_ref, hid_ref, *, alpha, tm, nct, off, ctx_len):
    hm = tm // 2
    halves = [slice(0, hm), slice(hm, tm)]

    def gla_branch_input(rows):
        o = of_ref[rows, :].astype(f32) + ob_ref[rows, :].astype(f32)
        heads = []
        for h in range(GLA_HEADS):
            oh = o[:, h * GLA_DV:(h + 1) * GLA_DV]
            ms = jnp.mean(oh * oh, axis=-1, keepdims=True)
            heads.append(oh * lax.rsqrt(ms + LN_EPS))
        of = jnp.concatenate(heads, axis=1) * gnw_ref[...]
        return (of * rs_ref[rows, :].astype(f32)).astype(bf16)

    yg_in = [gla_branch_input(rows) for rows in halves]
    branches = [(_dot(yg_in[i], wbg_ref[...]), _dot(pb_ref[rows, :], wbp_ref[...]))
                for i, rows in enumerate(halves)]
    merged = [(sgp_ref[rows, :].astype(f32) * branches[i][1]
               + sgg_ref[rows, :].astype(f32) * branches[i][0]).astype(bf16)
              for i, rows in enumerate(halves)]
    mix = [_dot(m, wout_ref[...]) for m in merged]
    x1 = [_layer_norm(alpha * x_ref[rows, :] + g1_ref[0] * mix[i]) * ln1w_ref[...] + ln1b_ref[...]
          for i, rows in enumerate(halves)]
    u2 = [(t * (1.0 + sc2_ref[0]) + sh2_ref[0]).astype(bf16) for t in x1]

    is_ctx = (pl.program_id(0) + off) < nct
    row_len = jnp.where(is_ctx, ctx_len, GRID_W)
    pos = lax.broadcasted_iota(jnp.int32, (hm, 1), 0) & (row_len - 1)
    has_prev = jnp.where(pos != 0, 1.0, 0.0)
    has_next = jnp.where(pos != row_len - 1, 1.0, 0.0)

    for i, rows in enumerate(halves):
        for c in range(FFN_HIDDEN // FFN_CHUNK):
            cs = slice(c * FFN_CHUNK, (c + 1) * FFN_CHUNK)
            a = _dot(u2[i], wup_ref[:, cs])
            gate = _dot(u2[i], wup_ref[:, FFN_HIDDEN + c * FFN_CHUNK:FFN_HIDDEN + (c + 1) * FFN_CHUNK])
            a_prev = pltpu.roll(a, 1, 0) * has_prev
            a_next = pltpu.roll(a, hm - 1, 0) * has_next
            a = cw_ref[0:1, cs] * a_prev + cw_ref[1:2, cs] * a + cw_ref[2:3, cs] * a_next + cb_ref[:, cs]
            hid = 0.5 * a * (1.0 + lax.erf(a * (2.0 ** -0.5))) * gate
            hid_ref[rows, cs] = hid.astype(bf16)
    down = [_dot(hid_ref[rows, :], wdn_ref[...]) for rows in halves]
    for i, rows in enumerate(halves):
        t = alpha * x1[i] + g2_ref[0] * down[i]
        out_ref[rows, :] = _layer_norm(t) * ln2w_ref[...] + ln2b_ref[...]


def _post(xs, o_f, o_b, rs, pb, sgp, sgg, mod, gnw, wbg, wbp, wout, ln1w, ln1b, wup, cw, cb, wdn, ln2w, ln2b, *,
          layer, tm, nct, tpb, nb, mod_rows, off, alpha, ctx_len):
    n, d = xs.shape
    nt = n // tm - off

    def mod_idx(slot):
        def idx(i):
            j = i + off
            row = jnp.where(j < nct, nb, jnp.maximum(j - nct, 0) // tpb)
            return ((layer * 6 + slot) * mod_rows + row, 0, 0)
        return idx

    tok = lambda w: pl.BlockSpec((tm, w), lambda i: (i + off, 0))
    const1 = pl.Buffered(1)
    vec = lambda w: pl.BlockSpec((None, 1, w), lambda i: (layer, 0, 0))
    mat = lambda a, b: pl.BlockSpec((None, a, b), lambda i: (layer, 0, 0), pipeline_mode=const1)
    modspec = lambda slot: pl.BlockSpec((1, 1, d), mod_idx(slot))
    return pl.pallas_call(
        functools.partial(_post_kernel, alpha=alpha, tm=tm, nct=nct, off=off, ctx_len=ctx_len),
        grid=(nt,),
        in_specs=[tok(d), tok(V_W), tok(V_W), tok(V_W), tok(POOL_WIDTH), tok(d), tok(d),
                  modspec(2), modspec(3), modspec(4), modspec(5),
                  vec(V_W), mat(V_W, d), mat(POOL_WIDTH, d), mat(d, d), vec(d), vec(d),
                  mat(d, 2 * FFN_HIDDEN), pl.BlockSpec((None, 3, FFN_HIDDEN), lambda i: (layer, 0, 0)),
                  vec(FFN_HIDDEN), mat(FFN_HIDDEN, d), vec(d), vec(d)],
        out_specs=pl.BlockSpec((tm, d), lambda i: (i, 0)),
        out_shape=jax.ShapeDtypeStruct((nt * tm, d), f32),
        scratch_shapes=[pltpu.VMEM((tm, FFN_HIDDEN), bf16)],
        compiler_params=pltpu.CompilerParams(
            dimension_semantics=("arbitrary",), vmem_limit_bytes=VMEM_LIMIT),
        name="post",
    )(xs, o_f, o_b, rs, pb, sgp, sgg, mod, mod, mod, mod, gnw, wbg, wbp, wout, ln1w, ln1b,
      wup, cw, cb, wdn, ln2w, ln2b)


def _pos_embed_2d(rows, cols, dim):
    quarter = dim // 4
    omega = 1.0 / (10000.0 ** (jnp.arange(quarter, dtype=f32) / quarter))
    r = jnp.arange(rows, dtype=f32)[:, None] * omega
    cl = jnp.arange(cols, dtype=f32)[:, None] * omega
    er = jnp.concatenate([jnp.sin(r), jnp.cos(r)], -1)
    ec = jnp.concatenate([jnp.sin(cl), jnp.cos(cl)], -1)
    emb = jnp.concatenate([jnp.broadcast_to(er[:, None, :], (rows, cols, dim // 2)),
                           jnp.broadcast_to(ec[None, :, :], (rows, cols, dim // 2))], -1)
    return emb.reshape(rows * cols, dim)


def _pool_tables(ctx_len):
    t = np.arange(POOL_BLOCK)
    mats = np.zeros((2, POOL_GROUPS, POOL_BLOCK, 2 * POOL_BLOCK), np.float32)
    cnts = np.zeros((2, POOL_BLOCK, 2 * POOL_WIDTH), np.float32)
    for kind, row_len in enumerate((GRID_W, ctx_len)):
        base = (t // row_len) * row_len
        r = t - base
        for g, w in enumerate(POOL_WINDOWS):
            lo = np.clip(r - w // 2, 0, row_len) + base
            hi = np.clip(r + w // 2, 0, row_len) + base
            band = (t[None, :] >= lo[:, None]) & (t[None, :] < hi[:, None])
            mats[kind, g] = np.concatenate([band, band], axis=1)
            cnts[kind, :, 2 * g * POOL_GS:2 * (g + 1) * POOL_GS] = (hi - lo)[:, None]
    return jnp.asarray(mats, bf16), jnp.asarray(cnts, f32)


def kernel(x, c, ctx, c_ctx, w_mod, b_mod, w_in, w_gate_f, b_gate_f, w_gate_b, b_gate_b, gla_norm_w, w_pool, pool_scale, w_br_pool, w_br_gla, w_out, ln1_w, ln1_b, w_up, conv_w, conv_b, w_down, ln2_w, ln2_b):
    nb, seq, d = x.shape
    ctx_len = ctx.shape[1]
    depth = w_mod.shape[0]
    tm = TOKEN_TILE
    assert d == D_MODEL and seq % GRID_W == 0
    assert ctx_len == POOL_BLOCK and ctx_len % GLA_BLOCK == 0 and seq % GLA_BLOCK == 0
    assert (nb * ctx_len) % tm == 0 and seq % tm == 0 and tm == 2 * POOL_BLOCK
    alpha = (2.0 * depth) ** 0.25
    nct = nb * ctx_len // tm
    tpb = seq // tm
    nbc, nbl = ctx_len // GLA_BLOCK, seq // GLA_BLOCK

    mod_rows = -(-(nb + 1) // 8) * 8
    cm = jnp.zeros((mod_rows, d), f32).at[:nb].set(c).at[nb].set(c_ctx)
    mod = _modulation(cm, w_mod, b_mod).reshape(depth * 6 * mod_rows, 1, d)

    zcols = w_in[:, :, C_GP:C_GP + Z_COPY].astype(bf16)
    w_in_p = (w_in[:, :, :C_GP].astype(bf16), w_in[:, :, C_GP + Z_COPY:].astype(bf16),
              jnp.concatenate([zcols, zcols, zcols, jnp.zeros((depth, d, LANES - 3 * Z_COPY), bf16)], axis=-1))
    wg32 = jnp.zeros((depth, Z_COPY, 2 * QK_W), f32)
    wg32 = wg32.at[:, :GLA_GATE_RANK, :QK_W].set(w_gate_f)
    wg32 = wg32.at[:, GLA_GATE_RANK:, QK_W:].set(w_gate_b)
    wg_hi = wg32.astype(bf16)
    wg_lo = (wg32 - wg_hi.astype(f32)).astype(bf16)
    wg = jnp.concatenate([wg_hi, wg_hi, wg_lo,
                          jnp.zeros((depth, LANES - 3 * Z_COPY, 2 * QK_W), bf16)], axis=1)
    bg = jnp.concatenate([b_gate_f, b_gate_b], axis=-1).reshape(depth, 1, 2 * QK_W)
    w_pool_b = w_pool.astype(bf16)
    wbp, wbg, wo = w_br_pool.astype(bf16), w_br_gla.astype(bf16), w_out.astype(bf16)
    wup, wdn = w_up.astype(bf16), w_down.astype(bf16)
    row3 = lambda a: a.reshape(depth, 1, a.shape[-1])
    pmats, cnts = _pool_tables(ctx_len)

    pos = _pos_embed_2d(seq // GRID_W, GRID_W, d)
    xs = _embed(ctx.reshape(nb * ctx_len, d), x.reshape(nb * seq, d), pos, tm)

    common = dict(tm=tm, nct=nct, tpb=tpb, nb=nb, mod_rows=mod_rows)
    for l in range(depth):
        off = nct if l == depth - 1 else 0
        pb, q, k, v, rs, sgp, sgg, z = _in_proj(xs, mod, w_in_p, pmats, cnts, w_pool_b, row3(pool_scale),
                                                  layer=l, **common)
        o_f, o_b = _gla(q, k, v, z, wg, bg, layer=l, nb=nb, nbc=nbc, nbl=nbl)
        xs = _post(xs, o_f, o_b, rs, pb, sgp, sgg, mod, row3(gla_norm_w), wbg, wbp, wo,
                   row3(ln1_w), row3(ln1_b), wup, conv_w, row3(conv_b), wdn, row3(ln2_w), row3(ln2_b),
                   layer=l, off=off, alpha=alpha, ctx_len=ctx_len, **common)
    return xs.reshape(nb, seq, d)
```

```python
import functools

import numpy as np
import jax
import jax.numpy as jnp
from jax import lax
from jax.experimental import pallas as pl
from jax.experimental.pallas import tpu as pltpu

f32 = jnp.float32
bf16 = jnp.bfloat16

D_MODEL = 1024
GRID_W = 64
POOL_WIDTH = 512
POOL_GROUPS = 4
POOL_GS = POOL_WIDTH // POOL_GROUPS
POOL_WINDOWS = (2, 4, 8, 16)
GLA_HEADS = 4
GLA_DK = 128
GLA_DV = 256
QK_W = GLA_HEADS * GLA_DK
V_W = GLA_HEADS * GLA_DV
GLA_GATE_RANK = 16
GLA_GATE_TEMP = 16.0
FFN_HIDDEN = 2816
LN_EPS = 1e-6

LANES = 128
POOL_BLOCK = 256
GLA_CHUNK = 128
GLA_BLOCK = 256
FFN_CHUNK = 256
PROJ_CHUNK = 256
TOKEN_TILE = 512
PAIR = 1
GLA_STAGE_AHEAD = 1
VMEM_LIMIT = 56 * 1024 * 1024

C_POOL, C_Q, C_K, C_V, C_R, C_GP, C_GG, C_Z = 0, 512, 1024, 1536, 2560, 3584, 4608, 5632
Z_COPY = 2 * GLA_GATE_RANK
LOG2E = 1.4426950408889634


def _sigmoid(t):
    return 1.0 / (1.0 + jnp.exp(-t))


def _layer_norm(t):
    mu = jnp.mean(t, axis=-1, keepdims=True)
    d = t - mu
    var = jnp.mean(d * d, axis=-1, keepdims=True)
    return d * lax.rsqrt(var + LN_EPS)


def _split_bf16(t):
    hi = t.astype(bf16)
    lo = (t - hi.astype(f32)).astype(bf16)
    return hi, lo


def _dot(a, b):
    return jnp.dot(a, b, preferred_element_type=f32)


def _mod_kernel(c_ref, w_ref, b_ref, o_ref):
    cs = c_ref[...]
    s = cs * _sigmoid(cs)
    o_ref[0, 0] = jnp.dot(s, w_ref[0], preferred_element_type=f32,
                          precision=lax.Precision.HIGHEST) + b_ref[0]


def _modulation(cm, w_mod, b_mod):
    depth, d, _ = w_mod.shape
    rows = cm.shape[0]
    return pl.pallas_call(
        _mod_kernel,
        grid=(depth, 6),
        in_specs=[
            pl.BlockSpec((rows, d), lambda l, s: (0, 0)),
            pl.BlockSpec((1, d, d), lambda l, s: (l, 0, s)),
            pl.BlockSpec((1, 1, d), lambda l, s: (l, 0, s)),
        ],
        out_specs=pl.BlockSpec((1, 1, rows, d), lambda l, s: (l, s, 0, 0)),
        out_shape=jax.ShapeDtypeStruct((depth, 6, rows, d), f32),
        compiler_params=pltpu.CompilerParams(
            dimension_semantics=("arbitrary", "arbitrary"), vmem_limit_bytes=VMEM_LIMIT),
        name="modulation",
    )(cm, w_mod, b_mod.reshape(depth, 1, 6 * d))


def _embed_kernel(ctx_ref, x_ref, pos_ref, o_ref, *, nct):
    i = pl.program_id(0)

    @pl.when(i < nct)
    def _():
        o_ref[...] = _layer_norm(ctx_ref[...])

    @pl.when(i >= nct)
    def _():
        o_ref[...] = _layer_norm(x_ref[...] + pos_ref[...])


def _embed(ctx_flat, x_flat, pos, tm):
    d = x_flat.shape[1]
    nct = ctx_flat.shape[0] // tm
    nlt = x_flat.shape[0] // tm
    npos = pos.shape[0] // tm
    nbatch = nlt // npos

    def lat(i):
        s = jnp.maximum(i - nct, 0)
        return (s % nbatch) * npos + s // nbatch

    return pl.pallas_call(
        functools.partial(_embed_kernel, nct=nct),
        grid=(nct + nlt,),
        in_specs=[
            pl.BlockSpec((tm, d), lambda i: (jnp.minimum(i, nct - 1), 0)),
            pl.BlockSpec((tm, d), lambda i: (lat(i), 0)),
            pl.BlockSpec((tm, d), lambda i: (jnp.maximum(i - nct, 0) // nbatch, 0)),
        ],
        out_specs=pl.BlockSpec((tm, d), lambda i: (jnp.where(i < nct, i, nct + lat(i)), 0)),
        out_shape=jax.ShapeDtypeStruct(((nct + nlt) * tm, d), f32),
        compiler_params=pltpu.CompilerParams(
            dimension_semantics=("arbitrary",), vmem_limit_bytes=VMEM_LIMIT),
        name="embed",
    )(ctx_flat, x_flat, pos)


def _in_proj_kernel(x_ref, sh_ref, sc_ref, wa_ref, wb_ref, wz_ref, pm_ref, cnt_ref, wpool_ref, pscale_ref,
                    pb_ref, q_ref, k_ref, v_ref, rs_ref, sgp_ref, sgg_ref, z_ref, *, tm):
    u = (x_ref[...] * (1.0 + sc_ref[0]) + sh_ref[0]).astype(bf16)

    def proj(a, n):
        if a >= C_Z:
            return _dot(u, wz_ref[:, a - C_Z:a - C_Z + n])
        if a >= C_GP:
            return _dot(u, wb_ref[:, a - C_GP:a - C_GP + n])
        return _dot(u, wa_ref[:, a:a + n])

    def segment(dst_ref, col0, width, fn):
        for j in range(width // PROJ_CHUNK):
            cs = slice(j * PROJ_CHUNK, (j + 1) * PROJ_CHUNK)
            dst_ref[:, cs] = fn(proj(col0 + j * PROJ_CHUNK, PROJ_CHUNK)).astype(bf16)

    gpc = PROJ_CHUNK // POOL_GS
    p_parts = [proj(C_POOL + j * PROJ_CHUNK, PROJ_CHUNK) for j in range(POOL_WIDTH // PROJ_CHUNK)]
    segment(q_ref, C_Q, QK_W, lambda t: t * (GLA_DK ** -0.5))
    segment(k_ref, C_K, QK_W, lambda t: t)
    pooled = []
    for g in range(POOL_GROUPS):
        gc = slice((g % gpc) * POOL_GS, (g % gpc + 1) * POOL_GS)
        pg = jnp.concatenate([p_parts[g // gpc][:POOL_BLOCK, gc], p_parts[g // gpc][POOL_BLOCK:, gc]], axis=1)
        hi, lo = _split_bf16(pg)
        s = _dot(pm_ref[0, g], jnp.concatenate([hi, lo], axis=0))
        pooled.append((s, pg))
    segment(v_ref, C_V, V_W, lambda t: t)
    for g in range(POOL_GROUPS):
        cols = slice(g * POOL_GS, (g + 1) * POOL_GS)
        s, pg = pooled[g]
        y = s / cnt_ref[0, :, 2 * g * POOL_GS:2 * (g + 1) * POOL_GS] - pg
        y = jnp.concatenate([y[:, :POOL_GS], y[:, POOL_GS:]], axis=0).astype(bf16)
        pb_ref[:, cols] = (_dot(y, wpool_ref[g]) * pscale_ref[:, cols]).astype(bf16)
    segment(rs_ref, C_R, V_W, lambda t: t * _sigmoid(t))
    segment(sgp_ref, C_GP, D_MODEL, _sigmoid)
    segment(sgg_ref, C_GG, D_MODEL, _sigmoid)
    z3 = proj(C_Z, LANES)
    z_hi, z_lo = _split_bf16(z3)
    lane = lax.broadcasted_iota(jnp.int32, z3.shape, 1)
    in_lo = (lane >= Z_COPY) & (lane < 2 * Z_COPY)
    z_ref[...] = jnp.where(in_lo, z_lo, z_hi)


def _in_proj(xs, mod, w_in_p, pmats, cnts, w_pool_b, pool_scale, *, layer, tm, nct, tpb, nb, mod_rows):
    n, d = xs.shape
    nt = n // tm

    def mod_idx(slot):
        def idx(i):
            row = jnp.where(i < nct, nb, jnp.maximum(i - nct, 0) // tpb)
            return ((layer * 6 + slot) * mod_rows + row, 0, 0)
        return idx

    def kind(i):
        return jnp.where(i < nct, 1, 0)

    tok = lambda w: pl.BlockSpec((tm, w), lambda i: (i, 0))
    const1 = pl.Buffered(1)
    out_shapes = (
        jax.ShapeDtypeStruct((n, POOL_WIDTH), bf16),
        jax.ShapeDtypeStruct((n, QK_W), bf16),
        jax.ShapeDtypeStruct((n, QK_W), bf16),
        jax.ShapeDtypeStruct((n, V_W), bf16),
        jax.ShapeDtypeStruct((n, V_W), bf16),
        jax.ShapeDtypeStruct((n, D_MODEL), bf16),
        jax.ShapeDtypeStruct((n, D_MODEL), bf16),
        jax.ShapeDtypeStruct((n, LANES), bf16),
    )
    return pl.pallas_call(
        functools.partial(_in_proj_kernel, tm=tm),
        grid=(nt,),
        in_specs=[
            tok(d),
            pl.BlockSpec((1, 1, d), mod_idx(0)),
            pl.BlockSpec((1, 1, d), mod_idx(1)),
            pl.BlockSpec((None, d, C_GP), lambda i: (layer, 0, 0), pipeline_mode=const1),
            pl.BlockSpec((None, d, C_Z - C_GP), lambda i: (layer, 0, 0), pipeline_mode=const1),
            pl.BlockSpec((None, d, LANES), lambda i: (layer, 0, 0), pipeline_mode=const1),
            pl.BlockSpec((1, POOL_GROUPS, POOL_BLOCK, 2 * POOL_BLOCK), lambda i: (kind(i), 0, 0, 0)),
            pl.BlockSpec((1, POOL_BLOCK, 2 * POOL_WIDTH), lambda i: (kind(i), 0, 0)),
            pl.BlockSpec((None, POOL_GROUPS, POOL_GS, POOL_GS), lambda i: (layer, 0, 0, 0)),
            pl.BlockSpec((None, 1, POOL_WIDTH), lambda i: (layer, 0, 0)),
        ],
        out_specs=[tok(POOL_WIDTH), tok(QK_W), tok(QK_W), tok(V_W), tok(V_W),
                   tok(D_MODEL), tok(D_MODEL), tok(LANES)],
        out_shape=out_shapes,
        compiler_params=pltpu.CompilerParams(
            dimension_semantics=("arbitrary",), vmem_limit_bytes=VMEM_LIMIT),
        name="in_proj",
    )(xs, mod, mod, *w_in_p, pmats, cnts, w_pool_b, pool_scale)


def _ordering_zero(t):
    bits = lax.bitcast_convert_type(t, jnp.uint32)
    bits = lax.shift_right_logical(lax.shift_right_logical(bits, jnp.uint32(16)), jnp.uint32(16))
    return lax.bitcast_convert_type(bits, f32)


def _gla_mask(reverse):
    c = GLA_CHUNK
    row = lax.broadcasted_iota(jnp.int32, (c, c), 0)
    col = lax.broadcasted_iota(jnp.int32, (c, c), 1)
    return (col >= row) if reverse else (col <= row)


def _gla_log_decay(pre, bias):
    pre = pre + bias
    scale = LOG2E / GLA_GATE_TEMP
    la = jnp.minimum(pre, 0.0) * scale - jnp.log(1.0 + jnp.exp(-jnp.abs(pre))) * scale
    la_hi, la_lo = _split_bf16(la)
    return jnp.concatenate([la_hi, la_lo], axis=0), _ordering_zero(la_lo[0:1, :].astype(f32))


def _gla_operands(q_ref, k_ref, rows, bc, after, *, reverse):
    c = GLA_CHUNK
    if reverse:
        b_end = bc[0:1, :]
        m = bc[c // 2:c // 2 + 1, :]
    else:
        b_end = bc[c - 1:c, :]
        m = bc[c // 2 - 1:c // 2, :]
    m = m + after
    qe = q_ref[rows, :] * jnp.exp2(bc - m).astype(bf16)
    ke = k_ref[rows, :] * jnp.exp2(m - bc).astype(bf16)
    q_full = qe * jnp.exp2(m).astype(bf16)
    k_end = ke * jnp.exp2(b_end - m).astype(bf16)
    ends = [b_end[:, h * GLA_DK:(h + 1) * GLA_DK] for h in range(GLA_HEADS)]
    end_rows = jnp.concatenate(ends + [jnp.zeros((GLA_DK - GLA_HEADS, GLA_DK), f32)], axis=0)
    dec_cols = jnp.exp2(end_rows.T)
    done = _ordering_zero((q_full[0:1, :] + k_end[0:1, :]).astype(f32))
    return (_gla_mask(reverse), qe, ke, q_full, k_end, dec_cols), done


def _gla_apply(prep, v_ref, o_ref, rows, states):
    c = GLA_CHUNK
    keep, qe, ke, q_full, k_end, dec_cols = prep
    heads = range(GLA_HEADS)
    kcs = [slice(h * GLA_DK, (h + 1) * GLA_DK) for h in heads]
    vcs = [slice(h * GLA_DV, (h + 1) * GLA_DV) for h in heads]
    att = [lax.dot_general(qe[:, kcs[h]], ke[:, kcs[h]], (((1,), (1,)), ((), ())), preferred_element_type=f32)
           for h in heads]
    kv = [_dot(k_end[:, kcs[h]].T, v_ref[rows, vcs[h]]) for h in heads]
    new_states = []
    for h in heads:
        a = jnp.where(keep, att[h], 0.0).astype(bf16)
        lhs = jnp.concatenate([a, q_full[:, kcs[h]]], axis=1)
        rhs = jnp.concatenate([v_ref[rows, vcs[h]], states[h].astype(bf16)], axis=0)
        o_ref[rows, vcs[h]] = _dot(lhs, rhs).astype(o_ref.dtype)
        new_states.append(states[h] * dec_cols[:, h:h + 1] + kv[h])
    return new_states


def _gla_kernel(qf_ref, kf_ref, vf_ref, zf_ref, qb_ref, kb_ref, vb_ref, zb_ref, wg_ref, bg_ref,
                of_ref, ob_ref, sf_ref, sb_ref, *, nchunks, nelem):
    @pl.when(pl.program_id(1) == 0)
    def _():
        sf_ref[...] = jnp.zeros_like(sf_ref)
        sb_ref[...] = jnp.zeros_like(sb_ref)

    c = GLA_CHUNK
    wg_f, wg_b = wg_ref[:, :QK_W], wg_ref[:, QK_W:]
    bias_f, bias_b = bg_ref[:, :QK_W], bg_ref[:, QK_W:]
    units = []
    for i in range(nchunks):
        for e in range(nelem):
            base = e * nchunks * c
            units.append(((e, False), False, slice(base + i * c, base + (i + 1) * c)))
            units.append(((e, True), True, slice(base + (nchunks - 1 - i) * c, base + (nchunks - i) * c)))
    refs = {False: (qf_ref, kf_ref, vf_ref, zf_ref, of_ref, wg_f, bias_f),
            True: (qb_ref, kb_ref, vb_ref, zb_ref, ob_ref, wg_b, bias_b)}
    state_refs = {False: sf_ref, True: sb_ref}
    states = {(e, r): [state_refs[r][e * GLA_HEADS + h] for h in range(GLA_HEADS)]
              for e in range(nelem) for r in (False, True)}
    tri2 = {r: jnp.concatenate([jnp.where(_gla_mask(r), 1.0, 0.0).astype(bf16)] * 2, axis=1) for r in (False, True)}

    pre = [_dot(refs[r][3][rows, :], refs[r][5]) for _, r, rows in units]
    n = len(units)
    bcs, preps = [None] * n, [None] * n
    la_done = jnp.zeros((1, QK_W), f32)
    op_done = jnp.zeros((1, QK_W), f32)

    las = [None] * n

    def log_decay(u):
        nonlocal la_done
        las[u], la_done = _gla_log_decay(pre[u], refs[units[u][1]][6] + la_done)

    def running_sum(u):
        bcs[u] = _dot(tri2[units[u][1]], las[u])

    def operands(u):
        nonlocal op_done
        _, r, rows = units[u]
        preps[u], op_done = _gla_operands(refs[r][0], refs[r][1], rows, bcs[u], op_done, reverse=r)

    ahead = GLA_STAGE_AHEAD
    for u in range(min(ahead + 2, n)):
        log_decay(u)
    for u in range(min(ahead + 1, n)):
        running_sum(u)
    for u in range(min(ahead, n)):
        operands(u)
    for u in range(n):
        key, r, rows = units[u]
        if u + ahead + 2 < n:
            log_decay(u + ahead + 2)
        if u + ahead + 1 < n:
            running_sum(u + ahead + 1)
        if u + ahead < n:
            operands(u + ahead)
        states[key] = _gla_apply(preps[u], refs[r][2], refs[r][4], rows, states[key])
    for (e, r), vals in states.items():
        for h in range(GLA_HEADS):
            state_refs[r][e * GLA_HEADS + h] = vals[h]


def _gla(q, k, v, z, wg, bg, *, layer, nb, nbc, nbl):
    n = q.shape[0]
    gb = PAIR * GLA_BLOCK
    groups = nb // PAIR
    assert nbc == 1

    def fwd_blk(p, s):
        return jnp.where(s < nbc, p, groups + p * nbl + (s - nbc))

    def bwd_blk(p, s):
        return jnp.where(s < nbc, p, groups + p * nbl + (nbl - 1 - (s - nbc)))

    def spec(w, blk):
        return pl.BlockSpec((gb, w), lambda p, s: (blk(p, s), 0))

    ins = []
    for blk in (fwd_blk, bwd_blk):
        ins += [spec(QK_W, blk), spec(QK_W, blk), spec(V_W, blk), spec(LANES, blk)]
    ins += [pl.BlockSpec((None, LANES, 2 * QK_W), lambda b, s: (layer, 0, 0)),
            pl.BlockSpec((None, 1, 2 * QK_W), lambda b, s: (layer, 0, 0))]
    return pl.pallas_call(
        functools.partial(_gla_kernel, nchunks=GLA_BLOCK // GLA_CHUNK, nelem=PAIR),
        grid=(groups, nbc + nbl),
        in_specs=ins,
        out_specs=[spec(V_W, fwd_blk), spec(V_W, bwd_blk)],
        out_shape=(jax.ShapeDtypeStruct((n, V_W), bf16), jax.ShapeDtypeStruct((n, V_W), bf16)),
        scratch_shapes=[pltpu.VMEM((PAIR * GLA_HEADS, GLA_DK, GLA_DV), f32),
                        pltpu.VMEM((PAIR * GLA_HEADS, GLA_DK, GLA_DV), f32)],
        compiler_params=pltpu.CompilerParams(
            dimension_semantics=("arbitrary", "arbitrary"), vmem_limit_bytes=VMEM_LIMIT),
        name="gla",
    )(q, k, v, z, q, k, v, z, wg, bg)


def _post_kernel(x_ref, of_ref, ob_ref, rs_ref, pb_ref, sgp_ref, sgg_ref, g1_ref, sh2_ref, sc2_ref, g2_ref,
                 gnw_ref, wbg_ref, wbp_ref, wout_ref, ln1w_ref, ln1b_ref,
                 wup_ref, cw_ref, cb_ref, wdn_ref, ln2w_ref, ln2b_ref,
                 out_ref, hid_ref, *, alpha, tm, nct, off, ctx_len):
    hm = tm // 2
    halves = [slice(0, hm), slice(hm, tm)]

    def gla_branch_input(rows):
        o = of_ref[rows, :].astype(f32) + ob_ref[rows, :].astype(f32)
        heads = []
        for h in range(GLA_HEADS):
            oh = o[:, h * GLA_DV:(h + 1) * GLA_DV]
            ms = jnp.mean(oh * oh, axis=-1, keepdims=True)
            heads.append(oh * lax.rsqrt(ms + LN_EPS))
        of = jnp.concatenate(heads, axis=1) * gnw_ref[...]
        return (of * rs_ref[rows, :].astype(f32)).astype(bf16)

    yg_in = [gla_branch_input(rows) for rows in halves]
    branches = [(_dot(yg_in[i], wbg_ref[...]), _dot(pb_ref[rows, :], wbp_ref[...]))
                for i, rows in enumerate(halves)]
    merged = [(sgp_ref[rows, :].astype(f32) * branches[i][1]
               + sgg_ref[rows, :].astype(f32) * branches[i][0]).astype(bf16)
              for i, rows in enumerate(halves)]
    mix = [_dot(m, wout_ref[...]) for m in merged]
    x1 = [_layer_norm(alpha * x_ref[rows, :] + g1_ref[0] * mix[i]) * ln1w_ref[...] + ln1b_ref[...]
          for i, rows in enumerate(halves)]
    u2 = [(t * (1.0 + sc2_ref[0]) + sh2_ref[0]).astype(bf16) for t in x1]

    is_ctx = (pl.program_id(0) + off) < nct
    row_len = jnp.where(is_ctx, ctx_len, GRID_W)
    pos = lax.broadcasted_iota(jnp.int32, (hm, 1), 0) & (row_len - 1)
    has_prev = jnp.where(pos != 0, 1.0, 0.0)
    has_next = jnp.where(pos != row_len - 1, 1.0, 0.0)

    for i, rows in enumerate(halves):
        for c in range(FFN_HIDDEN // FFN_CHUNK):
            cs = slice(c * FFN_CHUNK, (c + 1) * FFN_CHUNK)
            a = _dot(u2[i], wup_ref[:, cs])
            gate = _dot(u2[i], wup_ref[:, FFN_HIDDEN + c * FFN_CHUNK:FFN_HIDDEN + (c + 1) * FFN_CHUNK])
            a_prev = pltpu.roll(a, 1, 0) * has_prev
            a_next = pltpu.roll(a, hm - 1, 0) * has_next
            a = cw_ref[0:1, cs] * a_prev + cw_ref[1:2, cs] * a + cw_ref[2:3, cs] * a_next + cb_ref[:, cs]
            hid = 0.5 * a * (1.0 + lax.erf(a * (2.0 ** -0.5))) * gate
            hid_ref[rows, cs] = hid.astype(bf16)
    down = [_dot(hid_ref[rows, :], wdn_ref[...]) for rows in halves]
    for i, rows in enumerate(halves):
        t = alpha * x1[i] + g2_ref[0] * down[i]
        out_ref[rows, :] = _layer_norm(t) * ln2w_ref[...] + ln2b_ref[...]


def _post(xs, o_f, o_b, rs, pb, sgp, sgg, mod, gnw, wbg, wbp, wout, ln1w, ln1b, wup, cw, cb, wdn, ln2w, ln2b, *,
          layer, tm, nct, tpb, nb, mod_rows, off, alpha, ctx_len):
    n, d = xs.shape
    nt = n // tm - off

    def mod_idx(slot):
        def idx(i):
            j = i + off
            row = jnp.where(j < nct, nb, jnp.maximum(j - nct, 0) // tpb)
            return ((layer * 6 + slot) * mod_rows + row, 0, 0)
        return idx

    tok = lambda w: pl.BlockSpec((tm, w), lambda i: (i + off, 0))
    const1 = pl.Buffered(1)
    vec = lambda w: pl.BlockSpec((None, 1, w), lambda i: (layer, 0, 0))
    mat = lambda a, b: pl.BlockSpec((None, a, b), lambda i: (layer, 0, 0), pipeline_mode=const1)
    modspec = lambda slot: pl.BlockSpec((1, 1, d), mod_idx(slot))
    return pl.pallas_call(
        functools.partial(_post_kernel, alpha=alpha, tm=tm, nct=nct, off=off, ctx_len=ctx_len),
        grid=(nt,),
        in_specs=[tok(d), tok(V_W), tok(V_W), tok(V_W), tok(POOL_WIDTH), tok(d), tok(d),
                  modspec(2), modspec(3), modspec(4), modspec(5),
                  vec(V_W), mat(V_W, d), mat(POOL_WIDTH, d), mat(d, d), vec(d), vec(d),
                  mat(d, 2 * FFN_HIDDEN), pl.BlockSpec((None, 3, FFN_HIDDEN), lambda i: (layer, 0, 0)),
                  vec(FFN_HIDDEN), mat(FFN_HIDDEN, d), vec(d), vec(d)],
        out_specs=pl.BlockSpec((tm, d), lambda i: (i, 0)),
        out_shape=jax.ShapeDtypeStruct((nt * tm, d), f32),
        scratch_shapes=[pltpu.VMEM((tm, FFN_HIDDEN), bf16)],
        compiler_params=pltpu.CompilerParams(
            dimension_semantics=("arbitrary",), vmem_limit_bytes=VMEM_LIMIT),
        name="post",
    )(xs, o_f, o_b, rs, pb, sgp, sgg, mod, mod, mod, mod, gnw, wbg, wbp, wout, ln1w, ln1b,
      wup, cw, cb, wdn, ln2w, ln2b)


def _pos_embed_2d(rows, cols, dim):
    quarter = dim // 4
    omega = 1.0 / (10000.0 ** (jnp.arange(quarter, dtype=f32) / quarter))
    r = jnp.arange(rows, dtype=f32)[:, None] * omega
    cl = jnp.arange(cols, dtype=f32)[:, None] * omega
    er = jnp.concatenate([jnp.sin(r), jnp.cos(r)], -1)
    ec = jnp.concatenate([jnp.sin(cl), jnp.cos(cl)], -1)
    emb = jnp.concatenate([jnp.broadcast_to(er[:, None, :], (rows, cols, dim // 2)),
                           jnp.broadcast_to(ec[None, :, :], (rows, cols, dim // 2))], -1)
    return emb.reshape(rows * cols, dim)


def _pool_tables(ctx_len):
    t = np.arange(POOL_BLOCK)
    mats = np.zeros((2, POOL_GROUPS, POOL_BLOCK, 2 * POOL_BLOCK), np.float32)
    cnts = np.zeros((2, POOL_BLOCK, 2 * POOL_WIDTH), np.float32)
    for kind, row_len in enumerate((GRID_W, ctx_len)):
        base = (t // row_len) * row_len
        r = t - base
        for g, w in enumerate(POOL_WINDOWS):
            lo = np.clip(r - w // 2, 0, row_len) + base
            hi = np.clip(r + w // 2, 0, row_len) + base
            band = (t[None, :] >= lo[:, None]) & (t[None, :] < hi[:, None])
            mats[kind, g] = np.concatenate([band, band], axis=1)
            cnts[kind, :, 2 * g * POOL_GS:2 * (g + 1) * POOL_GS] = (hi - lo)[:, None]
    return jnp.asarray(mats, bf16), jnp.asarray(cnts, f32)


def kernel(x, c, ctx, c_ctx, w_mod, b_mod, w_in, w_gate_f, b_gate_f, w_gate_b, b_gate_b, gla_norm_w, w_pool, pool_scale, w_br_pool, w_br_gla, w_out, ln1_w, ln1_b, w_up, conv_w, conv_b, w_down, ln2_w, ln2_b):
    nb, seq, d = x.shape
    ctx_len = ctx.shape[1]
    depth = w_mod.shape[0]
    tm = TOKEN_TILE
    assert d == D_MODEL and seq % GRID_W == 0
    assert ctx_len == POOL_BLOCK and ctx_len % GLA_BLOCK == 0 and seq % GLA_BLOCK == 0
    assert (nb * ctx_len) % tm == 0 and seq % tm == 0 and tm == 2 * POOL_BLOCK
    alpha = (2.0 * depth) ** 0.25
    nct = nb * ctx_len // tm
    tpb = seq // tm
    nbc, nbl = ctx_len // GLA_BLOCK, seq // GLA_BLOCK

    mod_rows = -(-(nb + 1) // 8) * 8
    cm = jnp.zeros((mod_rows, d), f32).at[:nb].set(c).at[nb].set(c_ctx)
    mod = _modulation(cm, w_mod, b_mod).reshape(depth * 6 * mod_rows, 1, d)

    zcols = w_in[:, :, C_GP:C_GP + Z_COPY].astype(bf16)
    w_in_p = (w_in[:, :, :C_GP].astype(bf16), w_in[:, :, C_GP + Z_COPY:].astype(bf16),
              jnp.concatenate([zcols, zcols, zcols, jnp.zeros((depth, d, LANES - 3 * Z_COPY), bf16)], axis=-1))
    wg32 = jnp.zeros((depth, Z_COPY, 2 * QK_W), f32)
    wg32 = wg32.at[:, :GLA_GATE_RANK, :QK_W].set(w_gate_f)
    wg32 = wg32.at[:, GLA_GATE_RANK:, QK_W:].set(w_gate_b)
    wg_hi = wg32.astype(bf16)
    wg_lo = (wg32 - wg_hi.astype(f32)).astype(bf16)
    wg = jnp.concatenate([wg_hi, wg_hi, wg_lo,
                          jnp.zeros((depth, LANES - 3 * Z_COPY, 2 * QK_W), bf16)], axis=1)
    bg = jnp.concatenate([b_gate_f, b_gate_b], axis=-1).reshape(depth, 1, 2 * QK_W)
    w_pool_b = w_pool.astype(bf16)
    wbp, wbg, wo = w_br_pool.astype(bf16), w_br_gla.astype(bf16), w_out.astype(bf16)
    wup, wdn = w_up.astype(bf16), w_down.astype(bf16)
    row3 = lambda a: a.reshape(depth, 1, a.shape[-1])
    pmats, cnts = _pool_tables(ctx_len)

    pos = _pos_embed_2d(seq // GRID_W, GRID_W, d)
    xs = _embed(ctx.reshape(nb * ctx_len, d), x.reshape(nb * seq, d), pos, tm)

    common = dict(tm=tm, nct=nct, tpb=tpb, nb=nb, mod_rows=mod_rows)
    for l in range(depth):
        off = nct if l == depth - 1 else 0
        pb, q, k, v, rs, sgp, sgg, z = _in_proj(xs, mod, w_in_p, pmats, cnts, w_pool_b, row3(pool_scale),
                                                  layer=l, **common)
        o_f, o_b = _gla(q, k, v, z, wg, bg, layer=l, nb=nb, nbc=nbc, nbl=nbl)
        xs = _post(xs, o_f, o_b, rs, pb, sgp, sgg, mod, row3(gla_norm_w), wbg, wbp, wo,
                   row3(ln1_w), row3(ln1_b), wup, conv_w, row3(conv_b), wdn, row3(ln2_w), row3(ln2_b),
                   layer=l, off=off, alpha=alpha, ctx_len=ctx_len, **common)
    return xs.reshape(nb, seq, d)
```

```python
import functools

import numpy as np
import jax
import jax.numpy as jnp
from jax import lax
from jax.experimental import pallas as pl
from jax.experimental.pallas import tpu as pltpu

f32 = jnp.float32
bf16 = jnp.bfloat16

D_MODEL = 1024
GRID_W = 64
POOL_WIDTH = 512
POOL_GROUPS = 4
POOL_GS = POOL_WIDTH // POOL_GROUPS
POOL_WINDOWS = (2, 4, 8, 16)
GLA_HEADS = 4
GLA_DK = 128
GLA_DV = 256
QK_W = GLA_HEADS * GLA_DK
V_W = GLA_HEADS * GLA_DV
GLA_GATE_RANK = 16
GLA_GATE_TEMP = 16.0
FFN_HIDDEN = 2816
LN_EPS = 1e-6

LANES = 128
POOL_BLOCK = 256
GLA_CHUNK = 128
GLA_BLOCK = 256
FFN_CHUNK = 256
PROJ_CHUNK = 256
TOKEN_TILE = 512
PAIR = 2
GLA_STAGE_AHEAD = 1
VMEM_LIMIT = 56 * 1024 * 1024

C_POOL, C_Q, C_K, C_V, C_R, C_GP, C_GG, C_Z = 0, 512, 1024, 1536, 2560, 3584, 4608, 5632
Z_COPY = 2 * GLA_GATE_RANK
LOG2E = 1.4426950408889634


def _sigmoid(t):
    return 1.0 / (1.0 + jnp.exp(-t))


def _layer_norm(t):
    mu = jnp.mean(t, axis=-1, keepdims=True)
    d = t - mu
    var = jnp.mean(d * d, axis=-1, keepdims=True)
    return d * lax.rsqrt(var + LN_EPS)


def _split_bf16(t):
    hi = t.astype(bf16)
    lo = (t - hi.astype(f32)).astype(bf16)
    return hi, lo


def _dot(a, b):
    return jnp.dot(a, b, preferred_element_type=f32)


def _mod_kernel(c_ref, w_ref, b_ref, o_ref):
    cs = c_ref[...]
    s = cs * _sigmoid(cs)
    o_ref[0, 0] = jnp.dot(s, w_ref[0], preferred_element_type=f32,
                          precision=lax.Precision.HIGHEST) + b_ref[0]


def _modulation(cm, w_mod, b_mod):
    depth, d, _ = w_mod.shape
    rows = cm.shape[0]
    return pl.pallas_call(
        _mod_kernel,
        grid=(depth, 6),
        in_specs=[
            pl.BlockSpec((rows, d), lambda l, s: (0, 0)),
            pl.BlockSpec((1, d, d), lambda l, s: (l, 0, s)),
            pl.BlockSpec((1, 1, d), lambda l, s: (l, 0, s)),
        ],
        out_specs=pl.BlockSpec((1, 1, rows, d), lambda l, s: (l, s, 0, 0)),
        out_shape=jax.ShapeDtypeStruct((depth, 6, rows, d), f32),
        compiler_params=pltpu.CompilerParams(
            dimension_semantics=("arbitrary", "arbitrary"), vmem_limit_bytes=VMEM_LIMIT),
        name="modulation",
    )(cm, w_mod, b_mod.reshape(depth, 1, 6 * d))


def _embed_kernel(ctx_ref, x_ref, pos_ref, o_ref, *, nct):
    i = pl.program_id(0)

    @pl.when(i < nct)
    def _():
        o_ref[...] = _layer_norm(ctx_ref[...])

    @pl.when(i >= nct)
    def _():
        for e in range(PAIR):
            o_ref[e * GLA_BLOCK:(e + 1) * GLA_BLOCK, :] = _layer_norm(x_ref[0, e, 0] + pos_ref[...])


def _embed(ctx_flat, x, pos, tm):
    nb, seq, d = x.shape
    groups, nbl = nb // PAIR, seq // GLA_BLOCK
    nct = ctx_flat.shape[0] // tm
    nlt = groups * nbl
    x5 = x.reshape(groups, PAIR, nbl, GLA_BLOCK, d)

    def blk(i):
        return jnp.maximum(i - nct, 0) // groups

    def grp(i):
        return jnp.maximum(i - nct, 0) % groups

    return pl.pallas_call(
        functools.partial(_embed_kernel, nct=nct),
        grid=(nct + nlt,),
        in_specs=[
            pl.BlockSpec((tm, d), lambda i: (jnp.minimum(i, nct - 1), 0)),
            pl.BlockSpec((1, PAIR, 1, GLA_BLOCK, d), lambda i: (grp(i), 0, blk(i), 0, 0)),
            pl.BlockSpec((GLA_BLOCK, d), lambda i: (blk(i), 0)),
        ],
        out_specs=pl.BlockSpec((tm, d), lambda i: (jnp.where(i < nct, i, nct + grp(i) * nbl + blk(i)), 0)),
        out_shape=jax.ShapeDtypeStruct(((nct + nlt) * tm, d), f32),
        compiler_params=pltpu.CompilerParams(
            dimension_semantics=("arbitrary",), vmem_limit_bytes=VMEM_LIMIT),
        name="embed",
    )(ctx_flat, x5, pos)


def _mod_spec(layer, slot, e, *, nct, tpb, nb, mod_rows, off=0):
    def idx(i):
        j = i + off
        row = jnp.where(j < nct, nb, PAIR * (jnp.maximum(j - nct, 0) // tpb) + e)
        return ((layer * 6 + slot) * mod_rows + row, 0, 0)
    return pl.BlockSpec((1, 1, D_MODEL), idx)


def _in_proj_kernel(x_ref, *refs, tm):
    mods, refs = refs[:2 * PAIR], refs[2 * PAIR:]
    (wa_ref, wb_ref, wz_ref, pm_ref, cnt_ref, wpool_ref, pscale_ref,
     pb_ref, q_ref, k_ref, v_ref, rs_ref, sgp_ref, sgg_ref, z_ref) = refs
    u = jnp.concatenate(
        [x_ref[e * GLA_BLOCK:(e + 1) * GLA_BLOCK, :] * (1.0 + mods[2 * e + 1][0]) + mods[2 * e][0]
         for e in range(PAIR)], axis=0).astype(bf16)

    def proj(a, n):
        if a >= C_Z:
            return _dot(u, wz_ref[:, a - C_Z:a - C_Z + n])
        if a >= C_GP:
            return _dot(u, wb_ref[:, a - C_GP:a - C_GP + n])
        return _dot(u, wa_ref[:, a:a + n])

    def segment(dst_ref, col0, width, fn):
        for j in range(width // PROJ_CHUNK):
            cs = slice(j * PROJ_CHUNK, (j + 1) * PROJ_CHUNK)
            dst_ref[:, cs] = fn(proj(col0 + j * PROJ_CHUNK, PROJ_CHUNK)).astype(bf16)

    gpc = PROJ_CHUNK // POOL_GS
    p_parts = [proj(C_POOL + j * PROJ_CHUNK, PROJ_CHUNK) for j in range(POOL_WIDTH // PROJ_CHUNK)]
    segment(q_ref, C_Q, QK_W, lambda t: t * (GLA_DK ** -0.5))
    segment(k_ref, C_K, QK_W, lambda t: t)
    pooled = []
    for g in range(POOL_GROUPS):
        gc = slice((g % gpc) * POOL_GS, (g % gpc + 1) * POOL_GS)
        pg = jnp.concatenate([p_parts[g // gpc][:POOL_BLOCK, gc], p_parts[g // gpc][POOL_BLOCK:, gc]], axis=1)
        hi, lo = _split_bf16(pg)
        s = _dot(pm_ref[0, g], jnp.concatenate([hi, lo], axis=0))
        pooled.append((s, pg))
    segment(v_ref, C_V, V_W, lambda t: t)
    for g in range(POOL_GROUPS):
        cols = slice(g * POOL_GS, (g + 1) * POOL_GS)
        s, pg = pooled[g]
        y = s / cnt_ref[0, :, 2 * g * POOL_GS:2 * (g + 1) * POOL_GS] - pg
        y = jnp.concatenate([y[:, :POOL_GS], y[:, POOL_GS:]], axis=0).astype(bf16)
        pb_ref[:, cols] = (_dot(y, wpool_ref[g]) * pscale_ref[:, cols]).astype(bf16)
    segment(rs_ref, C_R, V_W, lambda t: t * _sigmoid(t))
    segment(sgp_ref, C_GP, D_MODEL, _sigmoid)
    segment(sgg_ref, C_GG, D_MODEL, _sigmoid)
    z3 = proj(C_Z, LANES)
    z_hi, z_lo = _split_bf16(z3)
    lane = lax.broadcasted_iota(jnp.int32, z3.shape, 1)
    in_lo = (lane >= Z_COPY) & (lane < 2 * Z_COPY)
    z_ref[...] = jnp.where(in_lo, z_lo, z_hi)


def _in_proj(xs, mod, w_in_p, pmats, cnts, w_pool_b, pool_scale, *, layer, tm, nct, tpb, nb, mod_rows):
    n, d = xs.shape
    nt = n // tm
    geo = dict(nct=nct, tpb=tpb, nb=nb, mod_rows=mod_rows)
    mod_specs = [_mod_spec(layer, slot, e, **geo) for e in range(PAIR) for slot in (0, 1)]

    def kind(i):
        return jnp.where(i < nct, 1, 0)

    tok = lambda w: pl.BlockSpec((tm, w), lambda i: (i, 0))
    const1 = pl.Buffered(1)
    out_shapes = (
        jax.ShapeDtypeStruct((n, POOL_WIDTH), bf16),
        jax.ShapeDtypeStruct((n, QK_W), bf16),
        jax.ShapeDtypeStruct((n, QK_W), bf16),
        jax.ShapeDtypeStruct((n, V_W), bf16),
        jax.ShapeDtypeStruct((n, V_W), bf16),
        jax.ShapeDtypeStruct((n, D_MODEL), bf16),
        jax.ShapeDtypeStruct((n, D_MODEL), bf16),
        jax.ShapeDtypeStruct((n, LANES), bf16),
    )
    return pl.pallas_call(
        functools.partial(_in_proj_kernel, tm=tm),
        grid=(nt,),
        in_specs=[
            tok(d),
            *mod_specs,
            pl.BlockSpec((None, d, C_GP), lambda i: (layer, 0, 0), pipeline_mode=const1),
            pl.BlockSpec((None, d, C_Z - C_GP), lambda i: (layer, 0, 0), pipeline_mode=const1),
            pl.BlockSpec((None, d, LANES), lambda i: (layer, 0, 0), pipeline_mode=const1),
            pl.BlockSpec((1, POOL_GROUPS, POOL_BLOCK, 2 * POOL_BLOCK), lambda i: (kind(i), 0, 0, 0)),
            pl.BlockSpec((1, POOL_BLOCK, 2 * POOL_WIDTH), lambda i: (kind(i), 0, 0)),
            pl.BlockSpec((None, POOL_GROUPS, POOL_GS, POOL_GS), lambda i: (layer, 0, 0, 0)),
            pl.BlockSpec((None, 1, POOL_WIDTH), lambda i: (layer, 0, 0)),
        ],
        out_specs=[tok(POOL_WIDTH), tok(QK_W), tok(QK_W), tok(V_W), tok(V_W),
                   tok(D_MODEL), tok(D_MODEL), tok(LANES)],
        out_shape=out_shapes,
        compiler_params=pltpu.CompilerParams(
            dimension_semantics=("arbitrary",), vmem_limit_bytes=VMEM_LIMIT),
        name="in_proj",
    )(xs, *[mod] * len(mod_specs), *w_in_p, pmats, cnts, w_pool_b, pool_scale)


def _ordering_zero(t):
    bits = lax.bitcast_convert_type(t, jnp.uint32)
    bits = lax.shift_right_logical(lax.shift_right_logical(bits, jnp.uint32(16)), jnp.uint32(16))
    return lax.bitcast_convert_type(bits, f32)


def _gla_mask(reverse):
    c = GLA_CHUNK
    row = lax.broadcasted_iota(jnp.int32, (c, c), 0)
    col = lax.broadcasted_iota(jnp.int32, (c, c), 1)
    return (col >= row) if reverse else (col <= row)


def _gla_log_decay(pre, bias):
    pre = pre + bias
    scale = LOG2E / GLA_GATE_TEMP
    la = jnp.minimum(pre, 0.0) * scale - jnp.log(1.0 + jnp.exp(-jnp.abs(pre))) * scale
    la_hi, la_lo = _split_bf16(la)
    return jnp.concatenate([la_hi, la_lo], axis=0), _ordering_zero(la_lo[0:1, :].astype(f32))


def _gla_operands(q_ref, k_ref, rows, bc, after, *, reverse):
    c = GLA_CHUNK
    if reverse:
        b_end = bc[0:1, :]
        m = bc[c // 2:c // 2 + 1, :]
    else:
        b_end = bc[c - 1:c, :]
        m = bc[c // 2 - 1:c // 2, :]
    m = m + after
    qe = q_ref[rows, :] * jnp.exp2(bc - m).astype(bf16)
    ke = k_ref[rows, :] * jnp.exp2(m - bc).astype(bf16)
    q_full = qe * jnp.exp2(m).astype(bf16)
    k_end = ke * jnp.exp2(b_end - m).astype(bf16)
    ends = [b_end[:, h * GLA_DK:(h + 1) * GLA_DK] for h in range(GLA_HEADS)]
    end_rows = jnp.concatenate(ends + [jnp.zeros((GLA_DK - GLA_HEADS, GLA_DK), f32)], axis=0)
    dec_cols = jnp.exp2(end_rows.T)
    done = _ordering_zero((q_full[0:1, :] + k_end[0:1, :]).astype(f32))
    return (_gla_mask(reverse), qe, ke, q_full, k_end, dec_cols), done


def _gla_apply(prep, v_ref, o_ref, rows, states):
    c = GLA_CHUNK
    keep, qe, ke, q_full, k_end, dec_cols = prep
    heads = range(GLA_HEADS)
    kcs = [slice(h * GLA_DK, (h + 1) * GLA_DK) for h in heads]
    vcs = [slice(h * GLA_DV, (h + 1) * GLA_DV) for h in heads]
    att = [lax.dot_general(qe[:, kcs[h]], ke[:, kcs[h]], (((1,), (1,)), ((), ())), preferred_element_type=f32)
           for h in heads]
    kv = [_dot(k_end[:, kcs[h]].T, v_ref[rows, vcs[h]]) for h in heads]
    new_states = []
    for h in heads:
        a = jnp.where(keep, att[h], 0.0).astype(bf16)
        lhs = jnp.concatenate([a, q_full[:, kcs[h]]], axis=1)
        rhs = jnp.concatenate([v_ref[rows, vcs[h]], states[h].astype(bf16)], axis=0)
        o_ref[rows, vcs[h]] = _dot(lhs, rhs).astype(o_ref.dtype)
        new_states.append(states[h] * dec_cols[:, h:h + 1] + kv[h])
    return new_states


def _gla_kernel(qf_ref, kf_ref, vf_ref, zf_ref, qb_ref, kb_ref, vb_ref, zb_ref, wg_ref, bg_ref,
                of_ref, ob_ref, sf_ref, sb_ref, *, nchunks, nelem):
    @pl.when(pl.program_id(1) == 0)
    def _():
        sf_ref[...] = jnp.zeros_like(sf_ref)
        sb_ref[...] = jnp.zeros_like(sb_ref)

    c = GLA_CHUNK
    wg_f, wg_b = wg_ref[:, :QK_W], wg_ref[:, QK_W:]
    bias_f, bias_b = bg_ref[:, :QK_W], bg_ref[:, QK_W:]
    units = []
    for i in range(nchunks):
        for e in range(nelem):
            base = e * nchunks * c
            units.append(((e, False), False, slice(base + i * c, base + (i + 1) * c)))
            units.append(((e, True), True, slice(base + (nchunks - 1 - i) * c, base + (nchunks - i) * c)))
    refs = {False: (qf_ref, kf_ref, vf_ref, zf_ref, of_ref, wg_f, bias_f),
            True: (qb_ref, kb_ref, vb_ref, zb_ref, ob_ref, wg_b, bias_b)}
    state_refs = {False: sf_ref, True: sb_ref}
    states = {(e, r): [state_refs[r][e * GLA_HEADS + h] for h in range(GLA_HEADS)]
              for e in range(nelem) for r in (False, True)}
    tri2 = {r: jnp.concatenate([jnp.where(_gla_mask(r), 1.0, 0.0).astype(bf16)] * 2, axis=1) for r in (False, True)}

    pre = [_dot(refs[r][3][rows, :], refs[r][5]) for _, r, rows in units]
    n = len(units)
    bcs, preps = [None] * n, [None] * n
    la_done = jnp.zeros((1, QK_W), f32)
    op_done = jnp.zeros((1, QK_W), f32)

    las = [None] * n

    def log_decay(u):
        nonlocal la_done
        las[u], la_done = _gla_log_decay(pre[u], refs[units[u][1]][6] + la_done)

    def running_sum(u):
        bcs[u] = _dot(tri2[units[u][1]], las[u])

    def operands(u):
        nonlocal op_done
        _, r, rows = units[u]
        preps[u], op_done = _gla_operands(refs[r][0], refs[r][1], rows, bcs[u], op_done, reverse=r)

    ahead = GLA_STAGE_AHEAD
    for u in range(min(ahead + 2, n)):
        log_decay(u)
    for u in range(min(ahead + 1, n)):
        running_sum(u)
    for u in range(min(ahead, n)):
        operands(u)
    for u in range(n):
        key, r, rows = units[u]
        if u + ahead + 2 < n:
            log_decay(u + ahead + 2)
        if u + ahead + 1 < n:
            running_sum(u + ahead + 1)
        if u + ahead < n:
            operands(u + ahead)
        states[key] = _gla_apply(preps[u], refs[r][2], refs[r][4], rows, states[key])
    for (e, r), vals in states.items():
        for h in range(GLA_HEADS):
            state_refs[r][e * GLA_HEADS + h] = vals[h]


def _gla(q, k, v, z, wg, bg, *, layer, nb, nbc, nbl):
    n = q.shape[0]
    gb = PAIR * GLA_BLOCK
    groups = nb // PAIR
    assert nbc == 1

    def fwd_blk(p, s):
        return jnp.where(s < nbc, p, groups + p * nbl + (s - nbc))

    def bwd_blk(p, s):
        return jnp.where(s < nbc, p, groups + p * nbl + (nbl - 1 - (s - nbc)))

    def spec(w, blk):
        return pl.BlockSpec((gb, w), lambda p, s: (blk(p, s), 0))

    ins = []
    for blk in (fwd_blk, bwd_blk):
        ins += [spec(QK_W, blk), spec(QK_W, blk), spec(V_W, blk), spec(LANES, blk)]
    ins += [pl.BlockSpec((None, LANES, 2 * QK_W), lambda b, s: (layer, 0, 0)),
            pl.BlockSpec((None, 1, 2 * QK_W), lambda b, s: (layer, 0, 0))]
    return pl.pallas_call(
        functools.partial(_gla_kernel, nchunks=GLA_BLOCK // GLA_CHUNK, nelem=PAIR),
        grid=(groups, nbc + nbl),
        in_specs=ins,
        out_specs=[spec(V_W, fwd_blk), spec(V_W, bwd_blk)],
        out_shape=(jax.ShapeDtypeStruct((n, V_W), bf16), jax.ShapeDtypeStruct((n, V_W), bf16)),
        scratch_shapes=[pltpu.VMEM((PAIR * GLA_HEADS, GLA_DK, GLA_DV), f32),
                        pltpu.VMEM((PAIR * GLA_HEADS, GLA_DK, GLA_DV), f32)],
        compiler_params=pltpu.CompilerParams(
            dimension_semantics=("arbitrary", "arbitrary"), vmem_limit_bytes=VMEM_LIMIT),
        name="gla",
    )(q, k, v, z, q, k, v, z, wg, bg)


def _post_kernel(x_ref, of_ref, ob_ref, rs_ref, pb_ref, sgp_ref, sgg_ref, *refs, alpha, tm, nct, off, ctx_len,
                 final):
    mods, refs = refs[:4 * PAIR], refs[4 * PAIR:]
    g1_refs, sh2_refs, sc2_refs, g2_refs = (mods[s::4] for s in range(4))
    (gnw_ref, wbg_ref, wbp_ref, wout_ref, ln1w_ref, ln1b_ref,
     wup_ref, cw_ref, cb_ref, wdn_ref, ln2w_ref, ln2b_ref, out_ref, hid_ref) = refs
    hm = tm // PAIR
    halves = [slice(e * hm, (e + 1) * hm) for e in range(PAIR)]

    def gla_branch_input(rows):
        o = of_ref[rows, :].astype(f32) + ob_ref[rows, :].astype(f32)
        heads = []
        for h in range(GLA_HEADS):
            oh = o[:, h * GLA_DV:(h + 1) * GLA_DV]
            ms = jnp.mean(oh * oh, axis=-1, keepdims=True)
            heads.append(oh * lax.rsqrt(ms + LN_EPS))
        of = jnp.concatenate(heads, axis=1) * gnw_ref[...]
        return (of * rs_ref[rows, :].astype(f32)).astype(bf16)

    branches = [(_dot(gla_branch_input(rows), wbg_ref[...]), _dot(pb_ref[rows, :], wbp_ref[...]))
                for rows in halves]
    merged = [(sgp_ref[rows, :].astype(f32) * branches[i][1]
               + sgg_ref[rows, :].astype(f32) * branches[i][0]).astype(bf16)
              for i, rows in enumerate(halves)]
    mix = [_dot(m, wout_ref[...]) for m in merged]
    x1 = [_layer_norm(alpha * x_ref[rows, :] + g1_refs[i][0] * mix[i]) * ln1w_ref[...] + ln1b_ref[...]
          for i, rows in enumerate(halves)]
    u2 = [(t * (1.0 + sc2_refs[i][0]) + sh2_refs[i][0]).astype(bf16) for i, t in enumerate(x1)]

    is_ctx = (pl.program_id(0) + off) < nct
    row_len = jnp.where(is_ctx, ctx_len, GRID_W)
    pos = lax.broadcasted_iota(jnp.int32, (hm, 1), 0) & (row_len - 1)
    has_prev = jnp.where(pos != 0, 1.0, 0.0)
    has_next = jnp.where(pos != row_len - 1, 1.0, 0.0)

    for i, rows in enumerate(halves):
        for c in range(FFN_HIDDEN // FFN_CHUNK):
            cs = slice(c * FFN_CHUNK, (c + 1) * FFN_CHUNK)
            a = _dot(u2[i], wup_ref[:, cs])
            gate = _dot(u2[i], wup_ref[:, FFN_HIDDEN + c * FFN_CHUNK:FFN_HIDDEN + (c + 1) * FFN_CHUNK])
            a_prev = pltpu.roll(a, 1, 0) * has_prev
            a_next = pltpu.roll(a, hm - 1, 0) * has_next
            a = cw_ref[0:1, cs] * a_prev + cw_ref[1:2, cs] * a + cw_ref[2:3, cs] * a_next + cb_ref[:, cs]
            hid = 0.5 * a * (1.0 + lax.erf(a * (2.0 ** -0.5))) * gate
            hid_ref[rows, cs] = hid.astype(bf16)
    down = [_dot(hid_ref[rows, :], wdn_ref[...]) for rows in halves]
    for i, rows in enumerate(halves):
        t = alpha * x1[i] + g2_refs[i][0] * down[i]
        y = _layer_norm(t) * ln2w_ref[...] + ln2b_ref[...]
        if final:
            out_ref[0, i, 0] = y
        else:
            out_ref[rows, :] = y


def _post(xs, o_f, o_b, rs, pb, sgp, sgg, mod, gnw, wbg, wbp, wout, ln1w, ln1b, wup, cw, cb, wdn, ln2w, ln2b, *,
          layer, tm, nct, tpb, nb, mod_rows, final, alpha, ctx_len):
    n, d = xs.shape
    off = nct if final else 0
    nt = n // tm - off
    geo = dict(nct=nct, tpb=tpb, nb=nb, mod_rows=mod_rows, off=off)
    mod_specs = [_mod_spec(layer, slot, e, **geo) for e in range(PAIR) for slot in (2, 3, 4, 5)]

    tok = lambda w: pl.BlockSpec((tm, w), lambda i: (i + off, 0))
    const1 = pl.Buffered(1)
    vec = lambda w: pl.BlockSpec((None, 1, w), lambda i: (layer, 0, 0))
    mat = lambda a, b: pl.BlockSpec((None, a, b), lambda i: (layer, 0, 0), pipeline_mode=const1)
    if final:
        out_spec = pl.BlockSpec((1, PAIR, 1, GLA_BLOCK, d), lambda i: (i // tpb, 0, i % tpb, 0, 0))
        out_shape = jax.ShapeDtypeStruct((nb // PAIR, PAIR, tpb, GLA_BLOCK, d), f32)
    else:
        out_spec = pl.BlockSpec((tm, d), lambda i: (i, 0))
        out_shape = jax.ShapeDtypeStruct((nt * tm, d), f32)
    return pl.pallas_call(
        functools.partial(_post_kernel, alpha=alpha, tm=tm, nct=nct, off=off, ctx_len=ctx_len, final=final),
        grid=(nt,),
        in_specs=[tok(d), tok(V_W), tok(V_W), tok(V_W), tok(POOL_WIDTH), tok(d), tok(d),
                  *mod_specs,
                  vec(V_W), mat(V_W, d), mat(POOL_WIDTH, d), mat(d, d), vec(d), vec(d),
                  mat(d, 2 * FFN_HIDDEN), pl.BlockSpec((None, 3, FFN_HIDDEN), lambda i: (layer, 0, 0)),
                  vec(FFN_HIDDEN), mat(FFN_HIDDEN, d), vec(d), vec(d)],
        out_specs=out_spec,
        out_shape=out_shape,
        scratch_shapes=[pltpu.VMEM((tm, FFN_HIDDEN), bf16)],
        compiler_params=pltpu.CompilerParams(
            dimension_semantics=("arbitrary",), vmem_limit_bytes=VMEM_LIMIT),
        name="post",
    )(xs, o_f, o_b, rs, pb, sgp, sgg, *[mod] * len(mod_specs), gnw, wbg, wbp, wout, ln1w, ln1b,
      wup, cw, cb, wdn, ln2w, ln2b)


def _pos_embed_2d(rows, cols, dim):
    quarter = dim // 4
    omega = 1.0 / (10000.0 ** (jnp.arange(quarter, dtype=f32) / quarter))
    r = jnp.arange(rows, dtype=f32)[:, None] * omega
    cl = jnp.arange(cols, dtype=f32)[:, None] * omega
    er = jnp.concatenate([jnp.sin(r), jnp.cos(r)], -1)
    ec = jnp.concatenate([jnp.sin(cl), jnp.cos(cl)], -1)
    emb = jnp.concatenate([jnp.broadcast_to(er[:, None, :], (rows, cols, dim // 2)),
                           jnp.broadcast_to(ec[None, :, :], (rows, cols, dim // 2))], -1)
    return emb.reshape(rows * cols, dim)


def _pool_tables(ctx_len):
    t = np.arange(POOL_BLOCK)
    mats = np.zeros((2, POOL_GROUPS, POOL_BLOCK, 2 * POOL_BLOCK), np.float32)
    cnts = np.zeros((2, POOL_BLOCK, 2 * POOL_WIDTH), np.float32)
    for kind, row_len in enumerate((GRID_W, ctx_len)):
        base = (t // row_len) * row_len
        r = t - base
        for g, w in enumerate(POOL_WINDOWS):
            lo = np.clip(r - w // 2, 0, row_len) + base
            hi = np.clip(r + w // 2, 0, row_len) + base
            band = (t[None, :] >= lo[:, None]) & (t[None, :] < hi[:, None])
            mats[kind, g] = np.concatenate([band, band], axis=1)
            cnts[kind, :, 2 * g * POOL_GS:2 * (g + 1) * POOL_GS] = (hi - lo)[:, None]
    return jnp.asarray(mats, bf16), jnp.asarray(cnts, f32)


def kernel(x, c, ctx, c_ctx, w_mod, b_mod, w_in, w_gate_f, b_gate_f, w_gate_b, b_gate_b, gla_norm_w, w_pool, pool_scale, w_br_pool, w_br_gla, w_out, ln1_w, ln1_b, w_up, conv_w, conv_b, w_down, ln2_w, ln2_b):
    nb, seq, d = x.shape
    ctx_len = ctx.shape[1]
    depth = w_mod.shape[0]
    tm = TOKEN_TILE
    assert d == D_MODEL and seq % GRID_W == 0
    assert ctx_len == POOL_BLOCK == GLA_BLOCK and seq % GLA_BLOCK == 0
    assert nb % PAIR == 0 and tm == PAIR * GLA_BLOCK == 2 * POOL_BLOCK
    alpha = (2.0 * depth) ** 0.25
    nct = nb * ctx_len // tm
    nbc, nbl = ctx_len // GLA_BLOCK, seq // GLA_BLOCK
    tpb = nbl

    mod_rows = -(-(nb + 1) // 8) * 8
    cm = jnp.zeros((mod_rows, d), f32).at[:nb].set(c).at[nb].set(c_ctx)
    mod = _modulation(cm, w_mod, b_mod).reshape(depth * 6 * mod_rows, 1, d)

    zcols = w_in[:, :, C_GP:C_GP + Z_COPY].astype(bf16)
    w_in_p = (w_in[:, :, :C_GP].astype(bf16), w_in[:, :, C_GP + Z_COPY:].astype(bf16),
              jnp.concatenate([zcols, zcols, zcols, jnp.zeros((depth, d, LANES - 3 * Z_COPY), bf16)], axis=-1))
    wg32 = jnp.zeros((depth, Z_COPY, 2 * QK_W), f32)
    wg32 = wg32.at[:, :GLA_GATE_RANK, :QK_W].set(w_gate_f)
    wg32 = wg32.at[:, GLA_GATE_RANK:, QK_W:].set(w_gate_b)
    wg_hi = wg32.astype(bf16)
    wg_lo = (wg32 - wg_hi.astype(f32)).astype(bf16)
    wg = jnp.concatenate([wg_hi, wg_hi, wg_lo,
                          jnp.zeros((depth, LANES - 3 * Z_COPY, 2 * QK_W), bf16)], axis=1)
    bg = jnp.concatenate([b_gate_f, b_gate_b], axis=-1).reshape(depth, 1, 2 * QK_W)
    w_pool_b = w_pool.astype(bf16)
    wbp, wbg, wo = w_br_pool.astype(bf16), w_br_gla.astype(bf16), w_out.astype(bf16)
    wup, wdn = w_up.astype(bf16), w_down.astype(bf16)
    row3 = lambda a: a.reshape(depth, 1, a.shape[-1])
    pmats, cnts = _pool_tables(ctx_len)

    pos = _pos_embed_2d(seq // GRID_W, GRID_W, d)
    xs = _embed(ctx.reshape(nb * ctx_len, d), x, pos, tm)

    common = dict(tm=tm, nct=nct, tpb=tpb, nb=nb, mod_rows=mod_rows)
    for l in range(depth):
        pb, q, k, v, rs, sgp, sgg, z = _in_proj(xs, mod, w_in_p, pmats, cnts, w_pool_b, row3(pool_scale),
                                                  layer=l, **common)
        o_f, o_b = _gla(q, k, v, z, wg, bg, layer=l, nb=nb, nbc=nbc, nbl=nbl)
        xs = _post(xs, o_f, o_b, rs, pb, sgp, sgg, mod, row3(gla_norm_w), wbg, wbp, wo,
                   row3(ln1_w), row3(ln1_b), wup, conv_w, row3(conv_b), wdn, row3(ln2_w), row3(ln2_b),
                   layer=l, final=l == depth - 1, alpha=alpha, ctx_len=ctx_len, **common)
    return xs.reshape(nb, seq, d)
```

```python
import functools

import numpy as np
import jax
import jax.numpy as jnp
from jax import lax
from jax.experimental import pallas as pl
from jax.experimental.pallas import tpu as pltpu

f32 = jnp.float32
bf16 = jnp.bfloat16

D_MODEL = 1024
GRID_W = 64
POOL_WIDTH = 512
POOL_GROUPS = 4
POOL_GS = POOL_WIDTH // POOL_GROUPS
POOL_WINDOWS = (2, 4, 8, 16)
GLA_HEADS = 4
GLA_DK = 128
GLA_DV = 256
QK_W = GLA_HEADS * GLA_DK
V_W = GLA_HEADS * GLA_DV
GLA_GATE_RANK = 16
GLA_GATE_TEMP = 16.0
FFN_HIDDEN = 2816
LN_EPS = 1e-6

LANES = 128
POOL_BLOCK = 256
GLA_CHUNK = 128
GLA_BLOCK = 256
FFN_CHUNK = 256
PROJ_CHUNK = 256
TOKEN_TILE = 512
PAIR = 2
GLA_STAGE_AHEAD = 3
VMEM_LIMIT = 56 * 1024 * 1024

C_POOL, C_Q, C_K, C_V, C_R, C_GP, C_GG, C_Z = 0, 512, 1024, 1536, 2560, 3584, 4608, 5632
Z_COPY = 2 * GLA_GATE_RANK
LOG2E = 1.4426950408889634


def _sigmoid(t):
    return 1.0 / (1.0 + jnp.exp(-t))


def _layer_norm(t):
    mu = jnp.mean(t, axis=-1, keepdims=True)
    d = t - mu
    var = jnp.mean(d * d, axis=-1, keepdims=True)
    return d * lax.rsqrt(var + LN_EPS)


def _split_bf16(t):
    hi = t.astype(bf16)
    lo = (t - hi.astype(f32)).astype(bf16)
    return hi, lo


def _dot(a, b):
    return jnp.dot(a, b, preferred_element_type=f32)


def _mod_kernel(c_ref, w_ref, b_ref, o_ref):
    cs = c_ref[...]
    s = cs * _sigmoid(cs)
    o_ref[0, 0] = jnp.dot(s, w_ref[0], preferred_element_type=f32,
                          precision=lax.Precision.HIGHEST) + b_ref[0]


def _modulation(cm, w_mod, b_mod):
    depth, d, _ = w_mod.shape
    rows = cm.shape[0]
    return pl.pallas_call(
        _mod_kernel,
        grid=(depth, 6),
        in_specs=[
            pl.BlockSpec((rows, d), lambda l, s: (0, 0)),
            pl.BlockSpec((1, d, d), lambda l, s: (l, 0, s)),
            pl.BlockSpec((1, 1, d), lambda l, s: (l, 0, s)),
        ],
        out_specs=pl.BlockSpec((1, 1, rows, d), lambda l, s: (l, s, 0, 0)),
        out_shape=jax.ShapeDtypeStruct((depth, 6, rows, d), f32),
        compiler_params=pltpu.CompilerParams(
            dimension_semantics=("arbitrary", "arbitrary"), vmem_limit_bytes=VMEM_LIMIT),
        name="modulation",
    )(cm, w_mod, b_mod.reshape(depth, 1, 6 * d))


def _embed_kernel(ctx_ref, x_ref, pos_ref, o_ref, *, nct):
    i = pl.program_id(0)

    @pl.when(i < nct)
    def _():
        o_ref[...] = _layer_norm(ctx_ref[...])

    @pl.when(i >= nct)
    def _():
        for e in range(PAIR):
            o_ref[e * GLA_BLOCK:(e + 1) * GLA_BLOCK, :] = _layer_norm(x_ref[0, e, 0] + pos_ref[...])


def _embed(ctx_flat, x, pos, tm):
    nb, seq, d = x.shape
    groups, nbl = nb // PAIR, seq // GLA_BLOCK
    nct = ctx_flat.shape[0] // tm
    nlt = groups * nbl
    x5 = x.reshape(groups, PAIR, nbl, GLA_BLOCK, d)

    def blk(i):
        return jnp.maximum(i - nct, 0) // groups

    def grp(i):
        return jnp.maximum(i - nct, 0) % groups

    return pl.pallas_call(
        functools.partial(_embed_kernel, nct=nct),
        grid=(nct + nlt,),
        in_specs=[
            pl.BlockSpec((tm, d), lambda i: (jnp.minimum(i, nct - 1), 0)),
            pl.BlockSpec((1, PAIR, 1, GLA_BLOCK, d), lambda i: (grp(i), 0, blk(i), 0, 0)),
            pl.BlockSpec((GLA_BLOCK, d), lambda i: (blk(i), 0)),
        ],
        out_specs=pl.BlockSpec((tm, d), lambda i: (jnp.where(i < nct, i, nct + grp(i) * nbl + blk(i)), 0)),
        out_shape=jax.ShapeDtypeStruct(((nct + nlt) * tm, d), f32),
        compiler_params=pltpu.CompilerParams(
            dimension_semantics=("arbitrary",), vmem_limit_bytes=VMEM_LIMIT),
        name="embed",
    )(ctx_flat, x5, pos)


def _mod_spec(layer, slot, e, *, nct, tpb, nb, mod_rows, off=0):
    def idx(i):
        j = i + off
        row = jnp.where(j < nct, nb, PAIR * (jnp.maximum(j - nct, 0) // tpb) + e)
        return ((layer * 6 + slot) * mod_rows + row, 0, 0)
    return pl.BlockSpec((1, 1, D_MODEL), idx)


def _in_proj_kernel(x_ref, *refs, tm):
    mods, refs = refs[:2 * PAIR], refs[2 * PAIR:]
    (wa_ref, wb_ref, wz_ref, pm_ref, cnt_ref, wpool_ref, pscale_ref,
     pb_ref, q_ref, k_ref, v_ref, rs_ref, sgp_ref, sgg_ref, z_ref) = refs
    u = jnp.concatenate(
        [x_ref[e * GLA_BLOCK:(e + 1) * GLA_BLOCK, :] * (1.0 + mods[2 * e + 1][0]) + mods[2 * e][0]
         for e in range(PAIR)], axis=0).astype(bf16)

    def proj(a, n):
        if a >= C_Z:
            return _dot(u, wz_ref[:, a - C_Z:a - C_Z + n])
        if a >= C_GP:
            return _dot(u, wb_ref[:, a - C_GP:a - C_GP + n])
        return _dot(u, wa_ref[:, a:a + n])

    def segment(dst_ref, col0, width, fn):
        for j in range(width // PROJ_CHUNK):
            cs = slice(j * PROJ_CHUNK, (j + 1) * PROJ_CHUNK)
            dst_ref[:, cs] = fn(proj(col0 + j * PROJ_CHUNK, PROJ_CHUNK)).astype(bf16)

    gpc = PROJ_CHUNK // POOL_GS
    p_parts = [proj(C_POOL + j * PROJ_CHUNK, PROJ_CHUNK) for j in range(POOL_WIDTH // PROJ_CHUNK)]
    segment(q_ref, C_Q, QK_W, lambda t: t * (GLA_DK ** -0.5))
    segment(k_ref, C_K, QK_W, lambda t: t)
    pooled = []
    for g in range(POOL_GROUPS):
        gc = slice((g % gpc) * POOL_GS, (g % gpc + 1) * POOL_GS)
        pg = jnp.concatenate([p_parts[g // gpc][:POOL_BLOCK, gc], p_parts[g // gpc][POOL_BLOCK:, gc]], axis=1)
        hi, lo = _split_bf16(pg)
        s = _dot(pm_ref[0, g], jnp.concatenate([hi, lo], axis=0))
        pooled.append((s, pg))
    segment(v_ref, C_V, V_W, lambda t: t)
    for g in range(POOL_GROUPS):
        cols = slice(g * POOL_GS, (g + 1) * POOL_GS)
        s, pg = pooled[g]
        y = s / cnt_ref[0, :, 2 * g * POOL_GS:2 * (g + 1) * POOL_GS] - pg
        y = jnp.concatenate([y[:, :POOL_GS], y[:, POOL_GS:]], axis=0).astype(bf16)
        pb_ref[:, cols] = (_dot(y, wpool_ref[g]) * pscale_ref[:, cols]).astype(bf16)
    segment(rs_ref, C_R, V_W, lambda t: t * _sigmoid(t))
    segment(sgp_ref, C_GP, D_MODEL, _sigmoid)
    segment(sgg_ref, C_GG, D_MODEL, _sigmoid)
    z3 = proj(C_Z, LANES)
    z_hi, z_lo = _split_bf16(z3)
    lane = lax.broadcasted_iota(jnp.int32, z3.shape, 1)
    in_lo = (lane >= Z_COPY) & (lane < 2 * Z_COPY)
    z_ref[...] = jnp.where(in_lo, z_lo, z_hi)


def _in_proj(xs, mod, w_in_p, pmats, cnts, w_pool_b, pool_scale, *, layer, tm, nct, tpb, nb, mod_rows):
    n, d = xs.shape
    nt = n // tm
    geo = dict(nct=nct, tpb=tpb, nb=nb, mod_rows=mod_rows)
    mod_specs = [_mod_spec(layer, slot, e, **geo) for e in range(PAIR) for slot in (0, 1)]

    def kind(i):
        return jnp.where(i < nct, 1, 0)

    tok = lambda w: pl.BlockSpec((tm, w), lambda i: (i, 0))
    const1 = pl.Buffered(1)
    out_shapes = (
        jax.ShapeDtypeStruct((n, POOL_WIDTH), bf16),
        jax.ShapeDtypeStruct((n, QK_W), bf16),
        jax.ShapeDtypeStruct((n, QK_W), bf16),
        jax.ShapeDtypeStruct((n, V_W), bf16),
        jax.ShapeDtypeStruct((n, V_W), bf16),
        jax.ShapeDtypeStruct((n, D_MODEL), bf16),
        jax.ShapeDtypeStruct((n, D_MODEL), bf16),
        jax.ShapeDtypeStruct((n, LANES), bf16),
    )
    return pl.pallas_call(
        functools.partial(_in_proj_kernel, tm=tm),
        grid=(nt,),
        in_specs=[
            tok(d),
            *mod_specs,
            pl.BlockSpec((None, d, C_GP), lambda i: (layer, 0, 0), pipeline_mode=const1),
            pl.BlockSpec((None, d, C_Z - C_GP), lambda i: (layer, 0, 0), pipeline_mode=const1),
            pl.BlockSpec((None, d, LANES), lambda i: (layer, 0, 0), pipeline_mode=const1),
            pl.BlockSpec((1, POOL_GROUPS, POOL_BLOCK, 2 * POOL_BLOCK), lambda i: (kind(i), 0, 0, 0)),
            pl.BlockSpec((1, POOL_BLOCK, 2 * POOL_WIDTH), lambda i: (kind(i), 0, 0)),
            pl.BlockSpec((None, POOL_GROUPS, POOL_GS, POOL_GS), lambda i: (layer, 0, 0, 0)),
            pl.BlockSpec((None, 1, POOL_WIDTH), lambda i: (layer, 0, 0)),
        ],
        out_specs=[tok(POOL_WIDTH), tok(QK_W), tok(QK_W), tok(V_W), tok(V_W),
                   tok(D_MODEL), tok(D_MODEL), tok(LANES)],
        out_shape=out_shapes,
        compiler_params=pltpu.CompilerParams(
            dimension_semantics=("arbitrary",), vmem_limit_bytes=VMEM_LIMIT),
        name="in_proj",
    )(xs, *[mod] * len(mod_specs), *w_in_p, pmats, cnts, w_pool_b, pool_scale)


def _ordering_zero(t):
    bits = lax.bitcast_convert_type(t, jnp.uint32)
    bits = lax.shift_right_logical(lax.shift_right_logical(bits, jnp.uint32(16)), jnp.uint32(16))
    return lax.bitcast_convert_type(bits, f32)


def _gla_mask(reverse):
    c = GLA_CHUNK
    row = lax.broadcasted_iota(jnp.int32, (c, c), 0)
    col = lax.broadcasted_iota(jnp.int32, (c, c), 1)
    return (col >= row) if reverse else (col <= row)


def _gla_log_decay(pre, bias):
    pre = pre + bias
    soft = jnp.log2(1.0 + jnp.exp2(jnp.abs(pre) * -LOG2E))
    la = jnp.minimum(pre, 0.0) * (LOG2E / GLA_GATE_TEMP) - soft * (1.0 / GLA_GATE_TEMP)
    la_hi, la_lo = _split_bf16(la)
    return jnp.concatenate([la_hi, la_lo], axis=0), _ordering_zero(la_lo[0:1, :].astype(f32))


def _gla_operands(q_ref, k_ref, rows, bc, after, *, reverse):
    c = GLA_CHUNK
    if reverse:
        b_end = bc[0:1, :]
        m = bc[c // 2:c // 2 + 1, :]
    else:
        b_end = bc[c - 1:c, :]
        m = bc[c // 2 - 1:c // 2, :]
    m = m + after
    qe = q_ref[rows, :] * jnp.exp2(bc - m).astype(bf16)
    ke = k_ref[rows, :] * jnp.exp2(m - bc).astype(bf16)
    q_full = qe * jnp.exp2(m).astype(bf16)
    k_end = ke * jnp.exp2(b_end - m).astype(bf16)
    ends = [b_end[:, h * GLA_DK:(h + 1) * GLA_DK] for h in range(GLA_HEADS)]
    end_rows = jnp.concatenate(ends + [jnp.zeros((GLA_DK - GLA_HEADS, GLA_DK), f32)], axis=0)
    dec_cols = jnp.exp2(end_rows.T)
    done = _ordering_zero((q_full[0:1, :] + k_end[0:1, :]).astype(f32))
    return (_gla_mask(reverse), qe, ke, q_full, k_end, dec_cols), done


def _gla_apply(prep, v_ref, o_ref, rows, states):
    c = GLA_CHUNK
    keep, qe, ke, q_full, k_end, dec_cols = prep
    heads = range(GLA_HEADS)
    kcs = [slice(h * GLA_DK, (h + 1) * GLA_DK) for h in heads]
    vcs = [slice(h * GLA_DV, (h + 1) * GLA_DV) for h in heads]
    att = [lax.dot_general(qe[:, kcs[h]], ke[:, kcs[h]], (((1,), (1,)), ((), ())), preferred_element_type=f32)
           for h in heads]
    kv = [_dot(k_end[:, kcs[h]].T, v_ref[rows, vcs[h]]) for h in heads]
    new_states = []
    for h in heads:
        a = jnp.where(keep, att[h], 0.0).astype(bf16)
        lhs = jnp.concatenate([a, q_full[:, kcs[h]]], axis=1)
        rhs = jnp.concatenate([v_ref[rows, vcs[h]], states[h].astype(bf16)], axis=0)
        o_ref[rows, vcs[h]] = _dot(lhs, rhs).astype(o_ref.dtype)
        new_states.append(states[h] * dec_cols[:, h:h + 1] + kv[h])
    return new_states


def _gla_kernel(qf_ref, kf_ref, vf_ref, zf_ref, qb_ref, kb_ref, vb_ref, zb_ref, wg_ref, bg_ref,
                of_ref, ob_ref, sf_ref, sb_ref, *, nchunks, nelem):
    @pl.when(pl.program_id(1) == 0)
    def _():
        sf_ref[...] = jnp.zeros_like(sf_ref)
        sb_ref[...] = jnp.zeros_like(sb_ref)

    c = GLA_CHUNK
    wg_f, wg_b = wg_ref[:, :QK_W], wg_ref[:, QK_W:]
    bias_f, bias_b = bg_ref[:, :QK_W], bg_ref[:, QK_W:]
    units = []
    for i in range(nchunks):
        for e in range(nelem):
            base = e * nchunks * c
            units.append(((e, False), False, slice(base + i * c, base + (i + 1) * c)))
            units.append(((e, True), True, slice(base + (nchunks - 1 - i) * c, base + (nchunks - i) * c)))
    refs = {False: (qf_ref, kf_ref, vf_ref, zf_ref, of_ref, wg_f, bias_f),
            True: (qb_ref, kb_ref, vb_ref, zb_ref, ob_ref, wg_b, bias_b)}
    state_refs = {False: sf_ref, True: sb_ref}
    states = {(e, r): [state_refs[r][e * GLA_HEADS + h] for h in range(GLA_HEADS)]
              for e in range(nelem) for r in (False, True)}
    tri2 = {r: jnp.concatenate([jnp.where(_gla_mask(r), 1.0, 0.0).astype(bf16)] * 2, axis=1) for r in (False, True)}

    pre = [_dot(refs[r][3][rows, :], refs[r][5]) for _, r, rows in units]
    n = len(units)
    bcs, preps = [None] * n, [None] * n
    la_done = jnp.zeros((1, QK_W), f32)
    op_done = jnp.zeros((1, QK_W), f32)

    las = [None] * n

    def log_decay(u):
        nonlocal la_done
        las[u], la_done = _gla_log_decay(pre[u], refs[units[u][1]][6] + la_done)

    def running_sum(u):
        bcs[u] = _dot(tri2[units[u][1]], las[u])

    def operands(u):
        nonlocal op_done
        _, r, rows = units[u]
        preps[u], op_done = _gla_operands(refs[r][0], refs[r][1], rows, bcs[u], op_done, reverse=r)

    ahead = GLA_STAGE_AHEAD
    for u in range(min(ahead + 2, n)):
        log_decay(u)
    for u in range(min(ahead + 1, n)):
        running_sum(u)
    for u in range(min(ahead, n)):
        operands(u)
    for u in range(n):
        key, r, rows = units[u]
        if u + ahead + 2 < n:
            log_decay(u + ahead + 2)
        if u + ahead + 1 < n:
            running_sum(u + ahead + 1)
        if u + ahead < n:
            operands(u + ahead)
        states[key] = _gla_apply(preps[u], refs[r][2], refs[r][4], rows, states[key])
    for (e, r), vals in states.items():
        for h in range(GLA_HEADS):
            state_refs[r][e * GLA_HEADS + h] = vals[h]


def _gla(q, k, v, z, wg, bg, *, layer, nb, nbc, nbl):
    n = q.shape[0]
    gb = PAIR * GLA_BLOCK
    groups = nb // PAIR
    assert nbc == 1

    def fwd_blk(p, s):
        return jnp.where(s < nbc, p, groups + p * nbl + (s - nbc))

    def bwd_blk(p, s):
        return jnp.where(s < nbc, p, groups + p * nbl + (nbl - 1 - (s - nbc)))

    def spec(w, blk):
        return pl.BlockSpec((gb, w), lambda p, s: (blk(p, s), 0))

    ins = []
    for blk in (fwd_blk, bwd_blk):
        ins += [spec(QK_W, blk), spec(QK_W, blk), spec(V_W, blk), spec(LANES, blk)]
    ins += [pl.BlockSpec((None, LANES, 2 * QK_W), lambda b, s: (layer, 0, 0)),
            pl.BlockSpec((None, 1, 2 * QK_W), lambda b, s: (layer, 0, 0))]
    return pl.pallas_call(
        functools.partial(_gla_kernel, nchunks=GLA_BLOCK // GLA_CHUNK, nelem=PAIR),
        grid=(groups, nbc + nbl),
        in_specs=ins,
        out_specs=[spec(V_W, fwd_blk), spec(V_W, bwd_blk)],
        out_shape=(jax.ShapeDtypeStruct((n, V_W), bf16), jax.ShapeDtypeStruct((n, V_W), bf16)),
        scratch_shapes=[pltpu.VMEM((PAIR * GLA_HEADS, GLA_DK, GLA_DV), f32),
                        pltpu.VMEM((PAIR * GLA_HEADS, GLA_DK, GLA_DV), f32)],
        compiler_params=pltpu.CompilerParams(
            dimension_semantics=("arbitrary", "arbitrary"), vmem_limit_bytes=VMEM_LIMIT),
        name="gla",
    )(q, k, v, z, q, k, v, z, wg, bg)


def _post_kernel(x_ref, of_ref, ob_ref, rs_ref, pb_ref, sgp_ref, sgg_ref, *refs, alpha, tm, nct, off, ctx_len,
                 final):
    mods, refs = refs[:4 * PAIR], refs[4 * PAIR:]
    g1_refs, sh2_refs, sc2_refs, g2_refs = (mods[s::4] for s in range(4))
    (gnw_ref, wbg_ref, wbp_ref, wout_ref, ln1w_ref, ln1b_ref,
     wup_ref, cw_ref, cb_ref, wdn_ref, ln2w_ref, ln2b_ref, out_ref, hid_ref) = refs
    hm = tm // PAIR
    halves = [slice(e * hm, (e + 1) * hm) for e in range(PAIR)]

    def gla_branch_input(rows):
        o = of_ref[rows, :].astype(f32) + ob_ref[rows, :].astype(f32)
        heads = []
        for h in range(GLA_HEADS):
            oh = o[:, h * GLA_DV:(h + 1) * GLA_DV]
            ms = jnp.mean(oh * oh, axis=-1, keepdims=True)
            heads.append(oh * lax.rsqrt(ms + LN_EPS))
        of = jnp.concatenate(heads, axis=1) * gnw_ref[...]
        return (of * rs_ref[rows, :].astype(f32)).astype(bf16)

    branches = [(_dot(gla_branch_input(rows), wbg_ref[...]), _dot(pb_ref[rows, :], wbp_ref[...]))
                for rows in halves]
    merged = [(sgp_ref[rows, :].astype(f32) * branches[i][1]
               + sgg_ref[rows, :].astype(f32) * branches[i][0]).astype(bf16)
              for i, rows in enumerate(halves)]
    mix = [_dot(m, wout_ref[...]) for m in merged]
    x1 = [_layer_norm(alpha * x_ref[rows, :] + g1_refs[i][0] * mix[i]) * ln1w_ref[...] + ln1b_ref[...]
          for i, rows in enumerate(halves)]
    u2 = [(t * (1.0 + sc2_refs[i][0]) + sh2_refs[i][0]).astype(bf16) for i, t in enumerate(x1)]

    is_ctx = (pl.program_id(0) + off) < nct
    row_len = jnp.where(is_ctx, ctx_len, GRID_W)
    pos = lax.broadcasted_iota(jnp.int32, (hm, 1), 0) & (row_len - 1)
    has_prev = jnp.where(pos != 0, 1.0, 0.0)
    has_next = jnp.where(pos != row_len - 1, 1.0, 0.0)

    for i, rows in enumerate(halves):
        for c in range(FFN_HIDDEN // FFN_CHUNK):
            cs = slice(c * FFN_CHUNK, (c + 1) * FFN_CHUNK)
            a = _dot(u2[i], wup_ref[:, cs])
            gate = _dot(u2[i], wup_ref[:, FFN_HIDDEN + c * FFN_CHUNK:FFN_HIDDEN + (c + 1) * FFN_CHUNK])
            a_prev = pltpu.roll(a, 1, 0) * has_prev
            a_next = pltpu.roll(a, hm - 1, 0) * has_next
            a = cw_ref[0:1, cs] * a_prev + cw_ref[1:2, cs] * a + cw_ref[2:3, cs] * a_next + cb_ref[:, cs]
            hid = 0.5 * a * (1.0 + lax.erf(a * (2.0 ** -0.5))) * gate
            hid_ref[rows, cs] = hid.astype(bf16)
    down = [_dot(hid_ref[rows, :], wdn_ref[...]) for rows in halves]
    for i, rows in enumerate(halves):
        t = alpha * x1[i] + g2_refs[i][0] * down[i]
        y = _layer_norm(t) * ln2w_ref[...] + ln2b_ref[...]
        if final:
            out_ref[0, i, 0] = y
        else:
            out_ref[rows, :] = y


def _post(xs, o_f, o_b, rs, pb, sgp, sgg, mod, gnw, wbg, wbp, wout, ln1w, ln1b, wup, cw, cb, wdn, ln2w, ln2b, *,
          layer, tm, nct, tpb, nb, mod_rows, final, alpha, ctx_len):
    n, d = xs.shape
    off = nct if final else 0
    nt = n // tm - off
    geo = dict(nct=nct, tpb=tpb, nb=nb, mod_rows=mod_rows, off=off)
    mod_specs = [_mod_spec(layer, slot, e, **geo) for e in range(PAIR) for slot in (2, 3, 4, 5)]

    tok = lambda w: pl.BlockSpec((tm, w), lambda i: (i + off, 0))
    const1 = pl.Buffered(1)
    vec = lambda w: pl.BlockSpec((None, 1, w), lambda i: (layer, 0, 0))
    mat = lambda a, b: pl.BlockSpec((None, a, b), lambda i: (layer, 0, 0), pipeline_mode=const1)
    if final:
        out_spec = pl.BlockSpec((1, PAIR, 1, GLA_BLOCK, d), lambda i: (i // tpb, 0, i % tpb, 0, 0))
        out_shape = jax.ShapeDtypeStruct((nb // PAIR, PAIR, tpb, GLA_BLOCK, d), f32)
    else:
        out_spec = pl.BlockSpec((tm, d), lambda i: (i, 0))
        out_shape = jax.ShapeDtypeStruct((nt * tm, d), f32)
    return pl.pallas_call(
        functools.partial(_post_kernel, alpha=alpha, tm=tm, nct=nct, off=off, ctx_len=ctx_len, final=final),
        grid=(nt,),
        in_specs=[tok(d), tok(V_W), tok(V_W), tok(V_W), tok(POOL_WIDTH), tok(d), tok(d),
                  *mod_specs,
                  vec(V_W), mat(V_W, d), mat(POOL_WIDTH, d), mat(d, d), vec(d), vec(d),
                  mat(d, 2 * FFN_HIDDEN), pl.BlockSpec((None, 3, FFN_HIDDEN), lambda i: (layer, 0, 0)),
                  vec(FFN_HIDDEN), mat(FFN_HIDDEN, d), vec(d), vec(d)],
        out_specs=out_spec,
        out_shape=out_shape,
        scratch_shapes=[pltpu.VMEM((tm, FFN_HIDDEN), bf16)],
        compiler_params=pltpu.CompilerParams(
            dimension_semantics=("arbitrary",), vmem_limit_bytes=VMEM_LIMIT),
        name="post",
    )(xs, o_f, o_b, rs, pb, sgp, sgg, *[mod] * len(mod_specs), gnw, wbg, wbp, wout, ln1w, ln1b,
      wup, cw, cb, wdn, ln2w, ln2b)


def _pos_embed_2d(rows, cols, dim):
    quarter = dim // 4
    omega = 1.0 / (10000.0 ** (jnp.arange(quarter, dtype=f32) / quarter))
    r = jnp.arange(rows, dtype=f32)[:, None] * omega
    cl = jnp.arange(cols, dtype=f32)[:, None] * omega
    er = jnp.concatenate([jnp.sin(r), jnp.cos(r)], -1)
    ec = jnp.concatenate([jnp.sin(cl), jnp.cos(cl)], -1)
    emb = jnp.concatenate([jnp.broadcast_to(er[:, None, :], (rows, cols, dim // 2)),
                           jnp.broadcast_to(ec[None, :, :], (rows, cols, dim // 2))], -1)
    return emb.reshape(rows * cols, dim)


def _pool_tables(ctx_len):
    t = np.arange(POOL_BLOCK)
    mats = np.zeros((2, POOL_GROUPS, POOL_BLOCK, 2 * POOL_BLOCK), np.float32)
    cnts = np.zeros((2, POOL_BLOCK, 2 * POOL_WIDTH), np.float32)
    for kind, row_len in enumerate((GRID_W, ctx_len)):
        base = (t // row_len) * row_len
        r = t - base
        for g, w in enumerate(POOL_WINDOWS):
            lo = np.clip(r - w // 2, 0, row_len) + base
            hi = np.clip(r + w // 2, 0, row_len) + base
            band = (t[None, :] >= lo[:, None]) & (t[None, :] < hi[:, None])
            mats[kind, g] = np.concatenate([band, band], axis=1)
            cnts[kind, :, 2 * g * POOL_GS:2 * (g + 1) * POOL_GS] = (hi - lo)[:, None]
    return jnp.asarray(mats, bf16), jnp.asarray(cnts, f32)


def kernel(x, c, ctx, c_ctx, w_mod, b_mod, w_in, w_gate_f, b_gate_f, w_gate_b, b_gate_b, gla_norm_w, w_pool, pool_scale, w_br_pool, w_br_gla, w_out, ln1_w, ln1_b, w_up, conv_w, conv_b, w_down, ln2_w, ln2_b):
    nb, seq, d = x.shape
    ctx_len = ctx.shape[1]
    depth = w_mod.shape[0]
    tm = TOKEN_TILE
    assert d == D_MODEL and seq % GRID_W == 0
    assert ctx_len == POOL_BLOCK == GLA_BLOCK and seq % GLA_BLOCK == 0
    assert nb % PAIR == 0 and tm == PAIR * GLA_BLOCK == 2 * POOL_BLOCK
    alpha = (2.0 * depth) ** 0.25
    nct = nb * ctx_len // tm
    nbc, nbl = ctx_len // GLA_BLOCK, seq // GLA_BLOCK
    tpb = nbl

    mod_rows = -(-(nb + 1) // 8) * 8
    cm = jnp.zeros((mod_rows, d), f32).at[:nb].set(c).at[nb].set(c_ctx)
    mod = _modulation(cm, w_mod, b_mod).reshape(depth * 6 * mod_rows, 1, d)

    w_in_b = w_in.astype(bf16)
    zcols = w_in_b[:, :, C_GP:C_GP + Z_COPY]
    w_in_p = (w_in_b[:, :, :C_GP], w_in_b[:, :, C_GP + Z_COPY:],
              jnp.concatenate([zcols, zcols, zcols, jnp.zeros((depth, d, LANES - 3 * Z_COPY), bf16)], axis=-1))
    wg32 = jnp.zeros((depth, Z_COPY, 2 * QK_W), f32)
    wg32 = wg32.at[:, :GLA_GATE_RANK, :QK_W].set(w_gate_f)
    wg32 = wg32.at[:, GLA_GATE_RANK:, QK_W:].set(w_gate_b)
    wg_hi = wg32.astype(bf16)
    wg_lo = (wg32 - wg_hi.astype(f32)).astype(bf16)
    wg = jnp.concatenate([wg_hi, wg_hi, wg_lo,
                          jnp.zeros((depth, LANES - 3 * Z_COPY, 2 * QK_W), bf16)], axis=1)
    bg = jnp.concatenate([b_gate_f, b_gate_b], axis=-1).reshape(depth, 1, 2 * QK_W)
    w_pool_b = w_pool.astype(bf16)
    wbp, wbg, wo = w_br_pool.astype(bf16), w_br_gla.astype(bf16), w_out.astype(bf16)
    wup, wdn = w_up.astype(bf16), w_down.astype(bf16)
    row3 = lambda a: a.reshape(depth, 1, a.shape[-1])
    pmats, cnts = _pool_tables(ctx_len)

    pos = _pos_embed_2d(seq // GRID_W, GRID_W, d)
    xs = _embed(ctx.reshape(nb * ctx_len, d), x, pos, tm)

    common = dict(tm=tm, nct=nct, tpb=tpb, nb=nb, mod_rows=mod_rows)
    for l in range(depth):
        pb, q, k, v, rs, sgp, sgg, z = _in_proj(xs, mod, w_in_p, pmats, cnts, w_pool_b, row3(pool_scale),
                                                  layer=l, **common)
        o_f, o_b = _gla(q, k, v, z, wg, bg, layer=l, nb=nb, nbc=nbc, nbl=nbl)
        xs = _post(xs, o_f, o_b, rs, pb, sgp, sgg, mod, row3(gla_norm_w), wbg, wbp, wo,
                   row3(ln1_w), row3(ln1_b), wup, conv_w, row3(conv_b), wdn, row3(ln2_w), row3(ln2_b),
                   layer=l, final=l == depth - 1, alpha=alpha, ctx_len=ctx_len, **common)
    return xs.reshape(nb, seq, d)
```

```python
import functools

import numpy as np
import jax
import jax.numpy as jnp
from jax import lax
from jax.experimental import pallas as pl
from jax.experimental.pallas import tpu as pltpu

f32 = jnp.float32
bf16 = jnp.bfloat16

D_MODEL = 1024
GRID_W = 64
POOL_WIDTH = 512
POOL_GROUPS = 4
POOL_GS = POOL_WIDTH // POOL_GROUPS
POOL_WINDOWS = (2, 4, 8, 16)
GLA_HEADS = 4
GLA_DK = 128
GLA_DV = 256
QK_W = GLA_HEADS * GLA_DK
V_W = GLA_HEADS * GLA_DV
GLA_GATE_RANK = 16
GLA_GATE_TEMP = 16.0
FFN_HIDDEN = 2816
LN_EPS = 1e-6

LANES = 128
POOL_BLOCK = 256
GLA_CHUNK = 128
GLA_BLOCK = 256
FFN_CHUNK = 256
PROJ_CHUNK = 256
TOKEN_TILE = 512
PAIR = 2
GLA_STAGE_AHEAD = 3
VMEM_LIMIT = 56 * 1024 * 1024

C_POOL, C_Q, C_K, C_V, C_R, C_GP, C_GG, C_Z = 0, 512, 1024, 1536, 2560, 3584, 4608, 5632
Z_COPY = 2 * GLA_GATE_RANK
A_Q, A_K, A_V, A_Z, A_WIDTH = 0, 512, 1024, 2048, 2176
M_RS, M_GP, M_GG, M_PB, M_WIDTH = 0, 1024, 2048, 3072, 3584
LOG2E = 1.4426950408889634


def _sigmoid(t):
    return 1.0 / (1.0 + jnp.exp(-t))


def _layer_norm(t):
    mu = jnp.mean(t, axis=-1, keepdims=True)
    d = t - mu
    var = jnp.mean(d * d, axis=-1, keepdims=True)
    return d * lax.rsqrt(var + LN_EPS)


def _split_bf16(t):
    hi = t.astype(bf16)
    lo = (t - hi.astype(f32)).astype(bf16)
    return hi, lo


def _dot(a, b):
    return jnp.dot(a, b, preferred_element_type=f32)


def _mod_kernel(c_ref, w_ref, b_ref, o_ref):
    cs = c_ref[...]
    s = cs * _sigmoid(cs)
    o_ref[0, 0] = jnp.dot(s, w_ref[0], preferred_element_type=f32,
                          precision=lax.Precision.HIGHEST) + b_ref[0]


def _modulation(cm, w_mod, b_mod):
    depth, d, _ = w_mod.shape
    rows = cm.shape[0]
    return pl.pallas_call(
        _mod_kernel,
        grid=(depth, 6),
        in_specs=[
            pl.BlockSpec((rows, d), lambda l, s: (0, 0)),
            pl.BlockSpec((1, d, d), lambda l, s: (l, 0, s)),
            pl.BlockSpec((1, 1, d), lambda l, s: (l, 0, s)),
        ],
        out_specs=pl.BlockSpec((1, 1, rows, d), lambda l, s: (l, s, 0, 0)),
        out_shape=jax.ShapeDtypeStruct((depth, 6, rows, d), f32),
        compiler_params=pltpu.CompilerParams(
            dimension_semantics=("arbitrary", "arbitrary"), vmem_limit_bytes=VMEM_LIMIT),
        name="modulation",
    )(cm, w_mod, b_mod.reshape(depth, 1, 6 * d))


def _embed_kernel(ctx_ref, x_ref, pos_ref, o_ref, *, nct):
    i = pl.program_id(0)

    @pl.when(i < nct)
    def _():
        o_ref[...] = _layer_norm(ctx_ref[...])

    @pl.when(i >= nct)
    def _():
        for e in range(PAIR):
            o_ref[e * GLA_BLOCK:(e + 1) * GLA_BLOCK, :] = _layer_norm(x_ref[0, e, 0] + pos_ref[...])


def _embed(ctx_flat, x, pos, tm):
    nb, seq, d = x.shape
    groups, nbl = nb // PAIR, seq // GLA_BLOCK
    nct = ctx_flat.shape[0] // tm
    nlt = groups * nbl
    x5 = x.reshape(groups, PAIR, nbl, GLA_BLOCK, d)

    def blk(i):
        return jnp.maximum(i - nct, 0) // groups

    def grp(i):
        return jnp.maximum(i - nct, 0) % groups

    return pl.pallas_call(
        functools.partial(_embed_kernel, nct=nct),
        grid=(nct + nlt,),
        in_specs=[
            pl.BlockSpec((tm, d), lambda i: (jnp.minimum(i, nct - 1), 0)),
            pl.BlockSpec((1, PAIR, 1, GLA_BLOCK, d), lambda i: (grp(i), 0, blk(i), 0, 0)),
            pl.BlockSpec((GLA_BLOCK, d), lambda i: (blk(i), 0)),
        ],
        out_specs=pl.BlockSpec((tm, d), lambda i: (jnp.where(i < nct, i, nct + grp(i) * nbl + blk(i)), 0)),
        out_shape=jax.ShapeDtypeStruct(((nct + nlt) * tm, d), f32),
        compiler_params=pltpu.CompilerParams(
            dimension_semantics=("arbitrary",), vmem_limit_bytes=VMEM_LIMIT),
        name="embed",
    )(ctx_flat, x5, pos)


def _mod_spec(layer, slot, e, *, nct, tpb, nb, mod_rows, off=0):
    def idx(i):
        j = i + off
        row = jnp.where(j < nct, nb, PAIR * (jnp.maximum(j - nct, 0) // tpb) + e)
        return ((layer * 6 + slot) * mod_rows + row, 0, 0)
    return pl.BlockSpec((1, 1, D_MODEL), idx)


def _in_proj_kernel(x_ref, *refs, tm):
    mods, refs = refs[:2 * PAIR], refs[2 * PAIR:]
    wa_ref, wb_ref, wz_ref, pm_ref, cnt_ref, wpool_ref, pscale_ref, att_ref, mix_ref = refs
    u = jnp.concatenate(
        [x_ref[e * GLA_BLOCK:(e + 1) * GLA_BLOCK, :] * (1.0 + mods[2 * e + 1][0]) + mods[2 * e][0]
         for e in range(PAIR)], axis=0).astype(bf16)

    def proj(a, n):
        if a >= C_Z:
            return _dot(u, wz_ref[:, a - C_Z:a - C_Z + n])
        if a >= C_GP:
            return _dot(u, wb_ref[:, a - C_GP:a - C_GP + n])
        return _dot(u, wa_ref[:, a:a + n])

    def segment(dst_ref, dst0, col0, width, fn):
        for j in range(width // PROJ_CHUNK):
            cs = slice(dst0 + j * PROJ_CHUNK, dst0 + (j + 1) * PROJ_CHUNK)
            dst_ref[:, cs] = fn(proj(col0 + j * PROJ_CHUNK, PROJ_CHUNK)).astype(bf16)

    gpc = PROJ_CHUNK // POOL_GS
    p_parts = [proj(C_POOL + j * PROJ_CHUNK, PROJ_CHUNK) for j in range(POOL_WIDTH // PROJ_CHUNK)]
    segment(att_ref, A_Q, C_Q, QK_W, lambda t: t * (GLA_DK ** -0.5))
    segment(att_ref, A_K, C_K, QK_W, lambda t: t)
    pooled = []
    for g in range(POOL_GROUPS):
        gc = slice((g % gpc) * POOL_GS, (g % gpc + 1) * POOL_GS)
        pg = jnp.concatenate([p_parts[g // gpc][:POOL_BLOCK, gc], p_parts[g // gpc][POOL_BLOCK:, gc]], axis=1)
        hi, lo = _split_bf16(pg)
        s = _dot(pm_ref[0, g], jnp.concatenate([hi, lo], axis=0))
        pooled.append((s, pg))
    segment(att_ref, A_V, C_V, V_W, lambda t: t)
    for g in range(POOL_GROUPS):
        cols = slice(g * POOL_GS, (g + 1) * POOL_GS)
        s, pg = pooled[g]
        y = s / cnt_ref[0, :, 2 * g * POOL_GS:2 * (g + 1) * POOL_GS] - pg
        y = jnp.concatenate([y[:, :POOL_GS], y[:, POOL_GS:]], axis=0).astype(bf16)
        mix_ref[:, M_PB + g * POOL_GS:M_PB + (g + 1) * POOL_GS] = (
            _dot(y, wpool_ref[g]) * pscale_ref[:, cols]).astype(bf16)
    segment(mix_ref, M_RS, C_R, V_W, lambda t: t * _sigmoid(t))
    segment(mix_ref, M_GP, C_GP, D_MODEL, _sigmoid)
    segment(mix_ref, M_GG, C_GG, D_MODEL, _sigmoid)
    z3 = proj(C_Z, LANES)
    z_hi, z_lo = _split_bf16(z3)
    lane = lax.broadcasted_iota(jnp.int32, z3.shape, 1)
    in_lo = (lane >= Z_COPY) & (lane < 2 * Z_COPY)
    att_ref[:, A_Z:A_Z + LANES] = jnp.where(in_lo, z_lo, z_hi)


def _in_proj(xs, mod, w_in_p, pmats, cnts, w_pool_b, pool_scale, *, layer, tm, nct, tpb, nb, mod_rows):
    n, d = xs.shape
    nt = n // tm
    geo = dict(nct=nct, tpb=tpb, nb=nb, mod_rows=mod_rows)
    mod_specs = [_mod_spec(layer, slot, e, **geo) for e in range(PAIR) for slot in (0, 1)]

    def kind(i):
        return jnp.where(i < nct, 1, 0)

    tok = lambda w: pl.BlockSpec((tm, w), lambda i: (i, 0))
    const1 = pl.Buffered(1)
    out_shapes = (jax.ShapeDtypeStruct((n, A_WIDTH), bf16), jax.ShapeDtypeStruct((n, M_WIDTH), bf16))
    return pl.pallas_call(
        functools.partial(_in_proj_kernel, tm=tm),
        grid=(nt,),
        in_specs=[
            tok(d),
            *mod_specs,
            pl.BlockSpec((None, d, C_GP), lambda i: (layer, 0, 0), pipeline_mode=const1),
            pl.BlockSpec((None, d, C_Z - C_GP), lambda i: (layer, 0, 0), pipeline_mode=const1),
            pl.BlockSpec((None, d, LANES), lambda i: (layer, 0, 0), pipeline_mode=const1),
            pl.BlockSpec((1, POOL_GROUPS, POOL_BLOCK, 2 * POOL_BLOCK), lambda i: (kind(i), 0, 0, 0)),
            pl.BlockSpec((1, POOL_BLOCK, 2 * POOL_WIDTH), lambda i: (kind(i), 0, 0)),
            pl.BlockSpec((None, POOL_GROUPS, POOL_GS, POOL_GS), lambda i: (layer, 0, 0, 0)),
            pl.BlockSpec((None, 1, POOL_WIDTH), lambda i: (layer, 0, 0)),
        ],
        out_specs=[tok(A_WIDTH), tok(M_WIDTH)],
        out_shape=out_shapes,
        compiler_params=pltpu.CompilerParams(
            dimension_semantics=("arbitrary",), vmem_limit_bytes=VMEM_LIMIT),
        name="in_proj",
    )(xs, *[mod] * len(mod_specs), *w_in_p, pmats, cnts, w_pool_b, pool_scale)


def _ordering_zero(t):
    bits = lax.bitcast_convert_type(t, jnp.uint32)
    bits = lax.shift_right_logical(lax.shift_right_logical(bits, jnp.uint32(16)), jnp.uint32(16))
    return lax.bitcast_convert_type(bits, f32)


def _gla_mask(reverse):
    c = GLA_CHUNK
    row = lax.broadcasted_iota(jnp.int32, (c, c), 0)
    col = lax.broadcasted_iota(jnp.int32, (c, c), 1)
    return (col >= row) if reverse else (col <= row)


def _gla_log_decay(pre, bias):
    pre = pre + bias
    soft = jnp.log2(1.0 + jnp.exp2(jnp.abs(pre) * -LOG2E))
    la = jnp.minimum(pre, 0.0) * (LOG2E / GLA_GATE_TEMP) - soft * (1.0 / GLA_GATE_TEMP)
    la_hi, la_lo = _split_bf16(la)
    return jnp.concatenate([la_hi, la_lo], axis=0), _ordering_zero(la_lo[0:1, :].astype(f32))


def _gla_operands(a_ref, rows, bc, after, *, reverse):
    c = GLA_CHUNK
    if reverse:
        b_end = bc[0:1, :]
        m = bc[c // 2:c // 2 + 1, :]
    else:
        b_end = bc[c - 1:c, :]
        m = bc[c // 2 - 1:c // 2, :]
    m = m + after
    qe = a_ref[rows, A_Q:A_Q + QK_W] * jnp.exp2(bc - m).astype(bf16)
    ke = a_ref[rows, A_K:A_K + QK_W] * jnp.exp2(m - bc).astype(bf16)
    q_full = qe * jnp.exp2(m).astype(bf16)
    k_end = ke * jnp.exp2(b_end - m).astype(bf16)
    ends = [b_end[:, h * GLA_DK:(h + 1) * GLA_DK] for h in range(GLA_HEADS)]
    end_rows = jnp.concatenate(ends + [jnp.zeros((GLA_DK - GLA_HEADS, GLA_DK), f32)], axis=0)
    dec_cols = jnp.exp2(end_rows.T)
    done = _ordering_zero((q_full[0:1, :] + k_end[0:1, :]).astype(f32))
    return (_gla_mask(reverse), qe, ke, q_full, k_end, dec_cols), done


def _gla_apply(prep, a_ref, o_ref, rows, states):
    c = GLA_CHUNK
    keep, qe, ke, q_full, k_end, dec_cols = prep
    heads = range(GLA_HEADS)
    kcs = [slice(h * GLA_DK, (h + 1) * GLA_DK) for h in heads]
    vcs = [slice(h * GLA_DV, (h + 1) * GLA_DV) for h in heads]
    att = [lax.dot_general(qe[:, kcs[h]], ke[:, kcs[h]], (((1,), (1,)), ((), ())), preferred_element_type=f32)
           for h in heads]
    v = [a_ref[rows, A_V + h * GLA_DV:A_V + (h + 1) * GLA_DV] for h in heads]
    kv = [_dot(k_end[:, kcs[h]].T, v[h]) for h in heads]
    new_states = []
    for h in heads:
        a = jnp.where(keep, att[h], 0.0).astype(bf16)
        lhs = jnp.concatenate([a, q_full[:, kcs[h]]], axis=1)
        rhs = jnp.concatenate([v[h], states[h].astype(bf16)], axis=0)
        o_ref[rows, vcs[h]] = _dot(lhs, rhs).astype(o_ref.dtype)
        new_states.append(states[h] * dec_cols[:, h:h + 1] + kv[h])
    return new_states


def _gla_kernel(af_ref, ab_ref, wg_ref, bg_ref, of_ref, ob_ref, sf_ref, sb_ref, *, nchunks, nelem):
    @pl.when(pl.program_id(1) == 0)
    def _():
        sf_ref[...] = jnp.zeros_like(sf_ref)
        sb_ref[...] = jnp.zeros_like(sb_ref)

    c = GLA_CHUNK
    wg_f, wg_b = wg_ref[:, :QK_W], wg_ref[:, QK_W:]
    bias_f, bias_b = bg_ref[:, :QK_W], bg_ref[:, QK_W:]
    units = []
    for i in range(nchunks):
        for e in range(nelem):
            base = e * nchunks * c
            units.append(((e, False), False, slice(base + i * c, base + (i + 1) * c)))
            units.append(((e, True), True, slice(base + (nchunks - 1 - i) * c, base + (nchunks - i) * c)))
    refs = {False: (af_ref, of_ref, wg_f, bias_f), True: (ab_ref, ob_ref, wg_b, bias_b)}
    state_refs = {False: sf_ref, True: sb_ref}
    states = {(e, r): [state_refs[r][e * GLA_HEADS + h] for h in range(GLA_HEADS)]
              for e in range(nelem) for r in (False, True)}
    tri2 = {r: jnp.concatenate([jnp.where(_gla_mask(r), 1.0, 0.0).astype(bf16)] * 2, axis=1) for r in (False, True)}

    pre = [_dot(refs[r][0][rows, A_Z:A_Z + LANES], refs[r][2]) for _, r, rows in units]
    n = len(units)
    bcs, preps = [None] * n, [None] * n
    la_done = jnp.zeros((1, QK_W), f32)
    op_done = jnp.zeros((1, QK_W), f32)

    las = [None] * n

    def log_decay(u):
        nonlocal la_done
        las[u], la_done = _gla_log_decay(pre[u], refs[units[u][1]][3] + la_done)

    def running_sum(u):
        bcs[u] = _dot(tri2[units[u][1]], las[u])

    def operands(u):
        nonlocal op_done
        _, r, rows = units[u]
        preps[u], op_done = _gla_operands(refs[r][0], rows, bcs[u], op_done, reverse=r)

    ahead = GLA_STAGE_AHEAD
    for u in range(min(ahead + 2, n)):
        log_decay(u)
    for u in range(min(ahead + 1, n)):
        running_sum(u)
    for u in range(min(ahead, n)):
        operands(u)
    for u in range(n):
        key, r, rows = units[u]
        if u + ahead + 2 < n:
            log_decay(u + ahead + 2)
        if u + ahead + 1 < n:
            running_sum(u + ahead + 1)
        if u + ahead < n:
            operands(u + ahead)
        states[key] = _gla_apply(preps[u], refs[r][0], refs[r][1], rows, states[key])
    for (e, r), vals in states.items():
        for h in range(GLA_HEADS):
            state_refs[r][e * GLA_HEADS + h] = vals[h]


def _gla(att_in, wg, bg, *, layer, nb, nbc, nbl):
    n = att_in.shape[0]
    gb = PAIR * GLA_BLOCK
    groups = nb // PAIR
    assert nbc == 1

    def fwd_blk(p, s):
        return jnp.where(s < nbc, p, groups + p * nbl + (s - nbc))

    def bwd_blk(p, s):
        return jnp.where(s < nbc, p, groups + p * nbl + (nbl - 1 - (s - nbc)))

    def spec(w, blk):
        return pl.BlockSpec((gb, w), lambda p, s: (blk(p, s), 0))

    ins = [spec(A_WIDTH, fwd_blk), spec(A_WIDTH, bwd_blk),
           pl.BlockSpec((None, LANES, 2 * QK_W), lambda b, s: (layer, 0, 0)),
           pl.BlockSpec((None, 1, 2 * QK_W), lambda b, s: (layer, 0, 0))]
    return pl.pallas_call(
        functools.partial(_gla_kernel, nchunks=GLA_BLOCK // GLA_CHUNK, nelem=PAIR),
        grid=(groups, nbc + nbl),
        in_specs=ins,
        out_specs=[spec(V_W, fwd_blk), spec(V_W, bwd_blk)],
        out_shape=(jax.ShapeDtypeStruct((n, V_W), bf16), jax.ShapeDtypeStruct((n, V_W), bf16)),
        scratch_shapes=[pltpu.VMEM((PAIR * GLA_HEADS, GLA_DK, GLA_DV), f32),
                        pltpu.VMEM((PAIR * GLA_HEADS, GLA_DK, GLA_DV), f32)],
        compiler_params=pltpu.CompilerParams(
            dimension_semantics=("arbitrary", "arbitrary"), vmem_limit_bytes=VMEM_LIMIT),
        name="gla",
    )(att_in, att_in, wg, bg)


def _post_kernel(x_ref, of_ref, ob_ref, mix_ref, *refs, alpha, tm, nct, off, ctx_len, final):
    mods, refs = refs[:4 * PAIR], refs[4 * PAIR:]
    g1_refs, sh2_refs, sc2_refs, g2_refs = (mods[s::4] for s in range(4))
    (gnw_ref, wbg_ref, wbp_ref, wout_ref, ln1w_ref, ln1b_ref,
     wup_ref, cw_ref, cb_ref, wdn_ref, ln2w_ref, ln2b_ref, out_ref, hid_ref) = refs
    hm = tm // PAIR
    halves = [slice(e * hm, (e + 1) * hm) for e in range(PAIR)]

    def gla_branch_input(rows):
        o = of_ref[rows, :].astype(f32) + ob_ref[rows, :].astype(f32)
        heads = []
        for h in range(GLA_HEADS):
            oh = o[:, h * GLA_DV:(h + 1) * GLA_DV]
            ms = jnp.mean(oh * oh, axis=-1, keepdims=True)
            heads.append(oh * lax.rsqrt(ms + LN_EPS))
        of = jnp.concatenate(heads, axis=1) * gnw_ref[...]
        return (of * mix_ref[rows, M_RS:M_RS + V_W].astype(f32)).astype(bf16)

    branches = [(_dot(gla_branch_input(rows), wbg_ref[...]), _dot(mix_ref[rows, M_PB:M_PB + POOL_WIDTH], wbp_ref[...]))
                for rows in halves]
    merged = [(mix_ref[rows, M_GP:M_GP + D_MODEL].astype(f32) * branches[i][1]
               + mix_ref[rows, M_GG:M_GG + D_MODEL].astype(f32) * branches[i][0]).astype(bf16)
              for i, rows in enumerate(halves)]
    mix = [_dot(m, wout_ref[...]) for m in merged]
    x1 = [_layer_norm(alpha * x_ref[rows, :] + g1_refs[i][0] * mix[i]) * ln1w_ref[...] + ln1b_ref[...]
          for i, rows in enumerate(halves)]
    u2 = [(t * (1.0 + sc2_refs[i][0]) + sh2_refs[i][0]).astype(bf16) for i, t in enumerate(x1)]

    is_ctx = (pl.program_id(0) + off) < nct
    row_len = jnp.where(is_ctx, ctx_len, GRID_W)
    pos = lax.broadcasted_iota(jnp.int32, (hm, 1), 0) & (row_len - 1)
    has_prev = jnp.where(pos != 0, 1.0, 0.0)
    has_next = jnp.where(pos != row_len - 1, 1.0, 0.0)

    for i, rows in enumerate(halves):
        for c in range(FFN_HIDDEN // FFN_CHUNK):
            cs = slice(c * FFN_CHUNK, (c + 1) * FFN_CHUNK)
            a = _dot(u2[i], wup_ref[:, cs])
            gate = _dot(u2[i], wup_ref[:, FFN_HIDDEN + c * FFN_CHUNK:FFN_HIDDEN + (c + 1) * FFN_CHUNK])
            a_prev = pltpu.roll(a, 1, 0) * has_prev
            a_next = pltpu.roll(a, hm - 1, 0) * has_next
            a = cw_ref[0:1, cs] * a_prev + cw_ref[1:2, cs] * a + cw_ref[2:3, cs] * a_next + cb_ref[:, cs]
            hid = 0.5 * a * (1.0 + lax.erf(a * (2.0 ** -0.5))) * gate
            hid_ref[rows, cs] = hid.astype(bf16)
    down = [_dot(hid_ref[rows, :], wdn_ref[...]) for rows in halves]
    for i, rows in enumerate(halves):
        t = alpha * x1[i] + g2_refs[i][0] * down[i]
        y = _layer_norm(t) * ln2w_ref[...] + ln2b_ref[...]
        if final:
            out_ref[0, i, 0] = y
        else:
            out_ref[rows, :] = y


def _post(xs, o_f, o_b, mix_in, mod, gnw, wbg, wbp, wout, ln1w, ln1b, wup, cw, cb, wdn, ln2w, ln2b, *,
          layer, tm, nct, tpb, nb, mod_rows, final, alpha, ctx_len):
    n, d = xs.shape
    off = nct if final else 0
    nt = n // tm - off
    geo = dict(nct=nct, tpb=tpb, nb=nb, mod_rows=mod_rows, off=off)
    mod_specs = [_mod_spec(layer, slot, e, **geo) for e in range(PAIR) for slot in (2, 3, 4, 5)]

    tok = lambda w: pl.BlockSpec((tm, w), lambda i: (i + off, 0))
    const1 = pl.Buffered(1)
    vec = lambda w: pl.BlockSpec((None, 1, w), lambda i: (layer, 0, 0))
    mat = lambda a, b: pl.BlockSpec((None, a, b), lambda i: (layer, 0, 0), pipeline_mode=const1)
    if final:
        out_spec = pl.BlockSpec((1, PAIR, 1, GLA_BLOCK, d), lambda i: (i // tpb, 0, i % tpb, 0, 0))
        out_shape = jax.ShapeDtypeStruct((nb // PAIR, PAIR, tpb, GLA_BLOCK, d), f32)
    else:
        out_spec = pl.BlockSpec((tm, d), lambda i: (i, 0))
        out_shape = jax.ShapeDtypeStruct((nt * tm, d), f32)
    return pl.pallas_call(
        functools.partial(_post_kernel, alpha=alpha, tm=tm, nct=nct, off=off, ctx_len=ctx_len, final=final),
        grid=(nt,),
        in_specs=[tok(d), tok(V_W), tok(V_W), tok(M_WIDTH),
                  *mod_specs,
                  vec(V_W), mat(V_W, d), mat(POOL_WIDTH, d), mat(d, d), vec(d), vec(d),
                  mat(d, 2 * FFN_HIDDEN), pl.BlockSpec((None, 3, FFN_HIDDEN), lambda i: (layer, 0, 0)),
                  vec(FFN_HIDDEN), mat(FFN_HIDDEN, d), vec(d), vec(d)],
        out_specs=out_spec,
        out_shape=out_shape,
        scratch_shapes=[pltpu.VMEM((tm, FFN_HIDDEN), bf16)],
        compiler_params=pltpu.CompilerParams(
            dimension_semantics=("arbitrary",), vmem_limit_bytes=VMEM_LIMIT),
        name="post",
    )(xs, o_f, o_b, mix_in, *[mod] * len(mod_specs), gnw, wbg, wbp, wout, ln1w, ln1b,
      wup, cw, cb, wdn, ln2w, ln2b)


def _pos_embed_2d(rows, cols, dim):
    quarter = dim // 4
    omega = 1.0 / (10000.0 ** (jnp.arange(quarter, dtype=f32) / quarter))
    r = jnp.arange(rows, dtype=f32)[:, None] * omega
    cl = jnp.arange(cols, dtype=f32)[:, None] * omega
    er = jnp.concatenate([jnp.sin(r), jnp.cos(r)], -1)
    ec = jnp.concatenate([jnp.sin(cl), jnp.cos(cl)], -1)
    emb = jnp.concatenate([jnp.broadcast_to(er[:, None, :], (rows, cols, dim // 2)),
                           jnp.broadcast_to(ec[None, :, :], (rows, cols, dim // 2))], -1)
    return emb.reshape(rows * cols, dim)


def _pool_tables(ctx_len):
    t = np.arange(POOL_BLOCK)
    mats = np.zeros((2, POOL_GROUPS, POOL_BLOCK, 2 * POOL_BLOCK), np.float32)
    cnts = np.zeros((2, POOL_BLOCK, 2 * POOL_WIDTH), np.float32)
    for kind, row_len in enumerate((GRID_W, ctx_len)):
        base = (t // row_len) * row_len
        r = t - base
        for g, w in enumerate(POOL_WINDOWS):
            lo = np.clip(r - w // 2, 0, row_len) + base
            hi = np.clip(r + w // 2, 0, row_len) + base
            band = (t[None, :] >= lo[:, None]) & (t[None, :] < hi[:, None])
            mats[kind, g] = np.concatenate([band, band], axis=1)
            cnts[kind, :, 2 * g * POOL_GS:2 * (g + 1) * POOL_GS] = (hi - lo)[:, None]
    return jnp.asarray(mats, bf16), jnp.asarray(cnts, f32)


def kernel(x, c, ctx, c_ctx, w_mod, b_mod, w_in, w_gate_f, b_gate_f, w_gate_b, b_gate_b, gla_norm_w, w_pool, pool_scale, w_br_pool, w_br_gla, w_out, ln1_w, ln1_b, w_up, conv_w, conv_b, w_down, ln2_w, ln2_b):
    nb, seq, d = x.shape
    ctx_len = ctx.shape[1]
    depth = w_mod.shape[0]
    tm = TOKEN_TILE
    assert d == D_MODEL and seq % GRID_W == 0
    assert ctx_len == POOL_BLOCK == GLA_BLOCK and seq % GLA_BLOCK == 0
    assert nb % PAIR == 0 and tm == PAIR * GLA_BLOCK == 2 * POOL_BLOCK
    alpha = (2.0 * depth) ** 0.25
    nct = nb * ctx_len // tm
    nbc, nbl = ctx_len // GLA_BLOCK, seq // GLA_BLOCK
    tpb = nbl

    mod_rows = -(-(nb + 1) // 8) * 8
    cm = jnp.zeros((mod_rows, d), f32).at[:nb].set(c).at[nb].set(c_ctx)
    mod = _modulation(cm, w_mod, b_mod).reshape(depth * 6 * mod_rows, 1, d)

    w_in_b = w_in.astype(bf16)
    zcols = w_in_b[:, :, C_GP:C_GP + Z_COPY]
    w_in_p = (w_in_b[:, :, :C_GP], w_in_b[:, :, C_GP + Z_COPY:],
              jnp.concatenate([zcols, zcols, zcols, jnp.zeros((depth, d, LANES - 3 * Z_COPY), bf16)], axis=-1))
    wg32 = jnp.zeros((depth, Z_COPY, 2 * QK_W), f32)
    wg32 = wg32.at[:, :GLA_GATE_RANK, :QK_W].set(w_gate_f)
    wg32 = wg32.at[:, GLA_GATE_RANK:, QK_W:].set(w_gate_b)
    wg_hi = wg32.astype(bf16)
    wg_lo = (wg32 - wg_hi.astype(f32)).astype(bf16)
    wg = jnp.concatenate([wg_hi, wg_hi, wg_lo,
                          jnp.zeros((depth, LANES - 3 * Z_COPY, 2 * QK_W), bf16)], axis=1)
    bg = jnp.concatenate([b_gate_f, b_gate_b], axis=-1).reshape(depth, 1, 2 * QK_W)
    w_pool_b = w_pool.astype(bf16)
    wbp, wbg, wo = w_br_pool.astype(bf16), w_br_gla.astype(bf16), w_out.astype(bf16)
    wup, wdn = w_up.astype(bf16), w_down.astype(bf16)
    row3 = lambda a: a.reshape(depth, 1, a.shape[-1])
    pmats, cnts = _pool_tables(ctx_len)

    pos = _pos_embed_2d(seq // GRID_W, GRID_W, d)
    xs = _embed(ctx.reshape(nb * ctx_len, d), x, pos, tm)

    common = dict(tm=tm, nct=nct, tpb=tpb, nb=nb, mod_rows=mod_rows)
    for l in range(depth):
        att_in, mix_in = _in_proj(xs, mod, w_in_p, pmats, cnts, w_pool_b, row3(pool_scale), layer=l, **common)
        o_f, o_b = _gla(att_in, wg, bg, layer=l, nb=nb, nbc=nbc, nbl=nbl)
        xs = _post(xs, o_f, o_b, mix_in, mod, row3(gla_norm_w), wbg, wbp, wo,
                   row3(ln1_w), row3(ln1_b), wup, conv_w, row3(conv_b), wdn, row3(ln2_w), row3(ln2_b),
                   layer=l, final=l == depth - 1, alpha=alpha, ctx_len=ctx_len, **common)
    return xs.reshape(nb, seq, d)
```
